```python
import math
import jax, jax.numpy as jnp
from jax import lax
import numpy as np

D_MODEL = 1024
BATCH = 1
SEQ = 16384
DEPTH = 4

HEAD_DIM = 128
N_HEADS = D_MODEL // HEAD_DIM
N_DIFF_HEADS = D_MODEL // (2 * HEAD_DIM)
D_FF = 2816
ROPE_THETA = 500000.0
ROT_DIM = HEAD_DIM // 4
Q_BLOCK = 128
N_MIXERS = 3
EPS = 1e-6
NEG = -1e30
SB_NEG = -1e4
N_SB = (DEPTH + 2) // 3
N_DIFF = (DEPTH + 1) // 3
N_FOX = DEPTH // 3

kernel_name = "hybrid_sb_diff_fox_macaron"

F32 = jnp.float32


def rms_norm(x, g):
    xf = x.astype(F32)
    y = xf * lax.rsqrt(jnp.mean(xf * xf, axis=-1, keepdims=True) + EPS)
    return (y * g.astype(F32)).astype(x.dtype)


def swiglu(x, w_gate, w_up, w_down):
    return (jax.nn.silu(x @ w_gate) * (x @ w_up)) @ w_down


def partial_rope(x, pos):
    half = ROT_DIM // 2
    inv = ROPE_THETA ** (-jnp.arange(half, dtype=F32) * 2.0 / ROT_DIM)
    ang = pos.astype(F32)[:, None] * inv[None, :]
    cos = jnp.cos(ang)[None, :, None, :]
    sin = jnp.sin(ang)[None, :, None, :]
    xr = x[..., :ROT_DIM].astype(F32)
    x1, x2 = xr[..., :half], xr[..., half:]
    rot = jnp.concatenate([x1 * cos - x2 * sin, x2 * cos + x1 * sin], axis=-1).astype(x.dtype)
    return jnp.concatenate([rot, x[..., ROT_DIM:]], axis=-1)


def to_heads(t, n_heads):
    B, S, W = t.shape
    return t.reshape(B, S, n_heads, W // n_heads).transpose(0, 2, 1, 3)


def from_heads(t):
    B, H, S, d = t.shape
    return t.transpose(0, 2, 1, 3).reshape(B, S, H * d)


def block_mask(i, n_keys, strict):
    qpos = i * Q_BLOCK + jnp.arange(Q_BLOCK)
    kpos = jnp.arange(n_keys)
    if strict:
        return kpos[None, :] < qpos[:, None]
    return kpos[None, :] <= qpos[:, None]


def rev_excl_cumsum(a, n_kb):
    a4 = a.reshape(*a.shape[:-1], n_kb, Q_BLOCK)
    ar = jnp.arange(Q_BLOCK)
    upper = (ar[None, :] > ar[:, None]).astype(F32)
    within = jnp.einsum('...nc,dc->...nd', a4, upper)
    br = jnp.arange(n_kb)
    upper_b = (br[None, :] > br[:, None]).astype(F32)
    later_blocks = jnp.einsum('...n,mn->...m', jnp.sum(a4, axis=-1), upper_b)
    return (within + later_blocks[..., None]).reshape(a.shape)


def stick_breaking_attention(q, k, v):
    S = q.shape[2]
    scale = HEAD_DIM ** -0.5
    outs = []
    for i in range(S // Q_BLOCK):
        n_keys = (i + 1) * Q_BLOCK
        qb = q[:, :, i * Q_BLOCK:n_keys]
        z = jnp.einsum('bhqd,bhkd->bhqk', qb, k[:, :, :n_keys], preferred_element_type=F32) * scale
        z = jnp.where(block_mask(i, n_keys, strict=True), z, SB_NEG)
        log_one_minus = jax.nn.log_sigmoid(-z)
        w = jnp.exp(z + log_one_minus + rev_excl_cumsum(log_one_minus, i + 1))
        outs.append(jnp.einsum('bhqk,bhkd->bhqd', w.astype(v.dtype), v[:, :, :n_keys]))
    return jnp.concatenate(outs, axis=2)


def differential_attention(q, k, v, lam):
    B, H2, S, _ = q.shape
    Hd = H2 // 2
    scale = HEAD_DIM ** -0.5
    outs = []
    for i in range(S // Q_BLOCK):
        n_keys = (i + 1) * Q_BLOCK
        qb = q[:, :, i * Q_BLOCK:n_keys]
        s = jnp.einsum('bhqd,bhkd->bhqk', qb, k[:, :, :n_keys], preferred_element_type=F32) * scale
        s = jnp.where(block_mask(i, n_keys, strict=False), s, NEG)
        p = jax.nn.softmax(s, axis=-1).reshape(B, Hd, 2, Q_BLOCK, n_keys)
        a = p[:, :, 0] - lam * p[:, :, 1]
        outs.append(jnp.einsum('bhqk,bhkd->bhqd', a.astype(v.dtype), v[:, :, :n_keys]))
    return jnp.concatenate(outs, axis=2)


def forgetting_attention(q, k, v, log_f):
    S = q.shape[2]
    scale = HEAD_DIM ** -0.5
    c = jnp.cumsum(log_f.astype(F32), axis=-1)
    outs = []
    for i in range(S // Q_BLOCK):
        n_keys = (i + 1) * Q_BLOCK
        qb = q[:, :, i * Q_BLOCK:n_keys]
        s = jnp.einsum('bhqd,bhkd->bhqk', qb, k[:, :, :n_keys], preferred_element_type=F32) * scale
        s = s + c[:, :, i * Q_BLOCK:n_keys, None] - c[:, :, None, :n_keys]
        s = jnp.where(block_mask(i, n_keys, strict=False), s, NEG)
        p = jax.nn.softmax(s, axis=-1)
        outs.append(jnp.einsum('bhqk,bhkd->bhqd', p.astype(v.dtype), v[:, :, :n_keys]))
    return jnp.concatenate(outs, axis=2)


def setup_inputs(seed: int = 0) -> dict:
    key = jax.random.key(seed)
    ks = jax.random.split(key, 13)
    D, F = D_MODEL, D_FF
    x = jax.random.normal(ks[0], (BATCH, SEQ, D), F32)
    norm_g = 1.0 + 0.02 * jax.random.normal(ks[1], (DEPTH, 3, D), F32)
    final_g = 1.0 + 0.02 * jax.random.normal(ks[2], (D,), F32)
    ffn_w_gate = jax.random.normal(ks[3], (DEPTH, 2, D, F), F32) * D ** -0.5
    ffn_w_up = jax.random.normal(ks[4], (DEPTH, 2, D, F), F32) * D ** -0.5
    ffn_w_down = jax.random.normal(ks[5], (DEPTH, 2, F, D), F32) * F ** -0.5
    w_qkv = jax.random.normal(ks[6], (DEPTH, D, 3 * D), F32) * D ** -0.5
    w_o = jax.random.normal(ks[7], (DEPTH, D, D), F32) * D ** -0.5
    diff_lambda = 0.1 * jax.random.normal(ks[8], (N_DIFF, 4, HEAD_DIM), F32)
    diff_subln_g = 1.0 + 0.02 * jax.random.normal(ks[9], (N_DIFF, 2 * HEAD_DIM), F32)
    fox_w_f = jax.random.normal(ks[10], (N_FOX, D, N_HEADS), F32) * 0.5 * D ** -0.5
    fox_b_f = 2.0 + 0.1 * jax.random.normal(ks[11], (N_FOX, N_HEADS), F32)
    return {"x": x, "norm_g": norm_g, "final_g": final_g,
            "ffn_w_gate": ffn_w_gate, "ffn_w_up": ffn_w_up, "ffn_w_down": ffn_w_down,
            "w_qkv": w_qkv, "w_o": w_o,
            "diff_lambda": diff_lambda, "diff_subln_g": diff_subln_g,
            "fox_w_f": fox_w_f, "fox_b_f": fox_b_f}


def reference(x, norm_g, final_g, ffn_w_gate, ffn_w_up, ffn_w_down, w_qkv, w_o,
              diff_lambda, diff_subln_g, fox_w_f, fox_b_f):
    S = x.shape[1]
    pos = jnp.arange(S)
    for i in range(DEPTH):
        kind = i % N_MIXERS
        j = i // N_MIXERS
        x = x + 0.5 * swiglu(rms_norm(x, norm_g[i, 0]), ffn_w_gate[i, 0], ffn_w_up[i, 0], ffn_w_down[i, 0])
        h = rms_norm(x, norm_g[i, 1])
        q, k, v = jnp.split(h @ w_qkv[i], 3, axis=-1)
        if kind == 0:
            mix = from_heads(stick_breaking_attention(
                to_heads(q, N_HEADS), to_heads(k, N_HEADS), to_heads(v, N_HEADS)))
        elif kind == 1:
            B_, S_, _ = q.shape
            qh = partial_rope(q.reshape(B_, S_, 2 * N_DIFF_HEADS, HEAD_DIM), pos).transpose(0, 2, 1, 3)
            kh = partial_rope(k.reshape(B_, S_, 2 * N_DIFF_HEADS, HEAD_DIM), pos).transpose(0, 2, 1, 3)
            vh = to_heads(v, N_DIFF_HEADS)
            lam_init = 0.8 - 0.6 * math.exp(-0.3 * i)
            lp = diff_lambda[j].astype(F32)
            lam = jnp.exp(jnp.sum(lp[0] * lp[1])) - jnp.exp(jnp.sum(lp[2] * lp[3])) + lam_init
            o = differential_attention(qh, kh, vh, lam)
            o = rms_norm(o, diff_subln_g[j]) * (1.0 - lam_init)
            mix = from_heads(o)
        else:
            log_f = jax.nn.log_sigmoid((h @ fox_w_f[j] + fox_b_f[j]).astype(F32))
            mix = from_heads(forgetting_attention(
                to_heads(q, N_HEADS), to_heads(k, N_HEADS), to_heads(v, N_HEADS),
                log_f.transpose(0, 2, 1)))
        x = x + mix @ w_o[i]
        x = x + 0.5 * swiglu(rms_norm(x, norm_g[i, 2]), ffn_w_gate[i, 1], ffn_w_up[i, 1], ffn_w_down[i, 1])
    return rms_norm(x, final_g)
```

```python
import functools
import math

import jax
import jax.numpy as jnp
from jax import lax
from jax.experimental import pallas as pl
from jax.experimental.pallas import tpu as pltpu

F32 = jnp.float32
BF16 = jnp.bfloat16

HEAD_DIM = 128
N_MIXERS = 3
ROPE_THETA = 500000.0
ROT_DIM = HEAD_DIM // 4
ROT_HALF = ROT_DIM // 2
EPS = 1e-6
NEG = -1e30
SB_NEG = -1e4
LOG2E = 1.4426950408889634
Q_SCALE = HEAD_DIM ** -0.5 * LOG2E

LANES = 128
MXU_DIM = 256
VMEM_BYTES = 64 * 1024 * 1024

ROW_TILE = 512
FFN_CHUNK = 256
CUM_BLOCK = MXU_DIM


def _dot(a, b):
    return jnp.dot(a, b, preferred_element_type=F32)


def _dot_nt(a, b):
    return lax.dot_general(a, b, (((1,), (1,)), ((), ())), preferred_element_type=F32)


def _rms(x, g):
    return x * lax.rsqrt(jnp.mean(x * x, axis=-1, keepdims=True) + EPS) * g


def _params(vmem_mib, n_axes):
    return pltpu.CompilerParams(
        dimension_semantics=("arbitrary",) * n_axes,
        vmem_limit_bytes=vmem_mib * 1024 * 1024)


def _resident(shape, index_map):
    return pl.BlockSpec(shape, index_map, pipeline_mode=pl.Buffered(1))


def _ffn_kernel(*refs, has_proj, has_final, n_chunks):
    it = iter(refs)
    x_ref = next(it)
    if has_proj:
        mix_ref, wo_ref = next(it), next(it)
    g_ref, wgu_ref, wd_ref = next(it), next(it), next(it)
    if has_final:
        fg_ref = next(it)
    o_ref, a_ref = next(it), next(it)

    x = x_ref[...]
    if has_proj:
        x = x + _dot(mix_ref[...], wo_ref[...])
    h = _rms(x, g_ref[...]).astype(BF16)
    c2 = 2 * FFN_CHUNK
    for c in range(n_chunks):
        gu = _dot(h, wgu_ref[:, c * c2:(c + 1) * c2])
        gate, up = gu[:, :FFN_CHUNK], gu[:, FFN_CHUNK:]
        a_ref[:, c * FFN_CHUNK:(c + 1) * FFN_CHUNK] = (gate * jax.nn.sigmoid(gate) * up).astype(BF16)
    y = x + 0.5 * _dot(a_ref[...], wd_ref[...])
    if has_final:
        y = _rms(y, fg_ref[...])
    o_ref[...] = y


def _ffn(x, g, wgu, wd, proj=None, final_g=None):
    s, d = x.shape
    f = wd.shape[0]
    tm = ROW_TILE
    row = lambda i: (i, 0)
    const = lambda i: (0, 0)
    args, specs = [x], [pl.BlockSpec((tm, d), row)]
    if proj is not None:
        mix, wo = proj
        args += [mix, wo]
        specs += [pl.BlockSpec((tm, d), row), _resident((d, d), const)]
    args += [g, wgu, wd]
    specs += [_resident((1, d), const), _resident((d, 2 * f), const), _resident((f, d), const)]
    if final_g is not None:
        args.append(final_g)
        specs.append(_resident((1, d), const))
    kern = functools.partial(_ffn_kernel, has_proj=proj is not None, has_final=final_g is not None,
                             n_chunks=f // FFN_CHUNK)
    return pl.pallas_call(
        kern, grid=(s // tm,), in_specs=specs, out_specs=pl.BlockSpec((tm, d), row),
        out_shape=jax.ShapeDtypeStruct((s, d), F32),
        scratch_shapes=[pltpu.VMEM((tm, f), BF16)],
        compiler_params=_params(48, 1), name="ffn")(*args)


def _qkv_kernel(*refs, kind, n_heads):
    it = iter(refs)
    x_ref, g_ref, wqt_ref, wk_ref, wvt_ref = (next(it) for _ in range(5))
    if kind == 1:
        cost_ref, sint_ref, cosf_ref, sina_ref, sinb_ref = (next(it) for _ in range(5))
    if kind == 2:
        wf_ref, bf_ref = next(it), next(it)
    qt_ref, k_ref, vt_ref = next(it), next(it), next(it)
    if kind == 2:
        cs_ref, ct_ref, carry_ref = next(it), next(it), next(it)

    h = _rms(x_ref[...], g_ref[...]).astype(BF16)
    qt = _dot_nt(wqt_ref[...], h) * Q_SCALE
    k = _dot(h, wk_ref[...])
    vt_ref[...] = _dot_nt(wvt_ref[...], h).astype(BF16)

    if kind != 1:
        qt_ref[...] = qt.astype(BF16)
        k_ref[...] = k.astype(BF16)
    else:
        cos_t, sin_t = cost_ref[...], sint_ref[...]
        cos_f, sin_a, sin_b = cosf_ref[...], sina_ref[...], sinb_ref[...]
        for hh in range(n_heads):
            r0 = hh * HEAD_DIM
            x1, x2 = qt[r0:r0 + ROT_HALF], qt[r0 + ROT_HALF:r0 + ROT_DIM]
            qt_ref[r0:r0 + ROT_HALF, :] = (x1 * cos_t - x2 * sin_t).astype(BF16)
            qt_ref[r0 + ROT_HALF:r0 + ROT_DIM, :] = (x2 * cos_t + x1 * sin_t).astype(BF16)
            qt_ref[r0 + ROT_DIM:r0 + HEAD_DIM, :] = qt[r0 + ROT_DIM:r0 + HEAD_DIM].astype(BF16)
            kh = k[:, r0:r0 + HEAD_DIM]
            rot = (kh * cos_f + pltpu.roll(kh, HEAD_DIM - ROT_HALF, 1) * sin_a
                   + pltpu.roll(kh, ROT_HALF, 1) * sin_b)
            k_ref[:, r0:r0 + HEAD_DIM] = rot.astype(BF16)

    if kind == 2:
        tm = h.shape[0]

        @pl.when(pl.program_id(0) == 0)
        def _():
            carry_ref[...] = jnp.zeros_like(carry_ref)

        fl = _dot(h, wf_ref[...]) + bf_ref[...]
        lf = jnp.minimum(fl, 0.0) - jnp.log(1.0 + jnp.exp(-jnp.abs(fl)))
        r = lax.broadcasted_iota(jnp.int32, (tm, tm), 0)
        c = lax.broadcasted_iota(jnp.int32, (tm, tm), 1)
        tri = jnp.where(c <= r, 1.0, 0.0).astype(BF16)
        p0 = lf.astype(BF16)
        r1 = lf - p0.astype(F32)
        p1 = r1.astype(BF16)
        p2 = (r1 - p1.astype(F32)).astype(BF16)
        csum = _dot(tri, p0) + _dot(tri, p1) + _dot(tri, p2) + carry_ref[...]
        carry_ref[...] = csum[tm - 1:tm, :]
        c2 = csum * LOG2E
        ct_ref[...] = c2.T[:n_heads, :]
        for hh in range(n_heads):
            cs_ref[hh] = jnp.broadcast_to(c2[:, hh:hh + 1], (tm, LANES))


def _qkv(x, g, wqt, wk, wvt, kind, rope=None, fox=None):
    s, d = x.shape
    tm = ROW_TILE
    nb = s // tm
    n_heads = d // HEAD_DIM
    row = lambda i: (i, 0)
    col = lambda i: (0, i)
    const = lambda i: (0, 0)
    args = [x, g, wqt, wk, wvt]
    specs = [pl.BlockSpec((tm, d), row), _resident((1, d), const),
             _resident((d, d), const), _resident((d, d), const), _resident((d, d), const)]
    out_shape = [jax.ShapeDtypeStruct((d, s), BF16), jax.ShapeDtypeStruct((s, d), BF16),
                 jax.ShapeDtypeStruct((nb, d, tm), BF16)]
    out_specs = [pl.BlockSpec((d, tm), col), pl.BlockSpec((tm, d), row),
                 pl.BlockSpec((None, d, tm), lambda i: (i, 0, 0))]
    scratch = []
    if kind == 1:
        args += list(rope)
        specs += [pl.BlockSpec((ROT_HALF, tm), col)] * 2 + [pl.BlockSpec((tm, HEAD_DIM), row)] * 3
    if kind == 2:
        args += list(fox)
        specs += [_resident((d, LANES), const), _resident((1, LANES), const)]
        out_shape += [jax.ShapeDtypeStruct((n_heads, s, LANES), F32),
                      jax.ShapeDtypeStruct((n_heads, s), F32)]
        out_specs += [pl.BlockSpec((n_heads, tm, LANES), lambda i: (0, i, 0)),
                      pl.BlockSpec((n_heads, tm), col)]
        scratch = [pltpu.VMEM((1, LANES), F32)]
    kern = functools.partial(_qkv_kernel, kind=kind, n_heads=n_heads)
    return pl.pallas_call(
        kern, grid=(nb,), in_specs=specs, out_specs=out_specs, out_shape=out_shape,
        scratch_shapes=scratch, compiler_params=_params(40, 1), name=f"qkv{kind}")(*args)


def _tile_iotas(tk, tq):
    return (lax.broadcasted_iota(jnp.int32, (tk, tq), 0),
            lax.broadcasted_iota(jnp.int32, (tk, tq), 1))


def _walk_causal(tile, qi):
    tile(qi, True)

    def body(i, carry):
        tile(qi - 1 - i, False)
        return carry

    lax.fori_loop(0, qi, body, 0)


def _sb_kernel(qt_ref, k_ref, vt_ref, o_ref, acc_ref, run_ref):
    qi = pl.program_id(1)
    tq = qt_ref.shape[1]
    tk = vt_ref.shape[2]
    qt = qt_ref[...]
    d_i, c_i = _tile_iotas(CUM_BLOCK, CUM_BLOCK)
    later = jnp.where(c_i > d_i, 1.0, 0.0).astype(BF16)
    acc_ref[...] = jnp.zeros_like(acc_ref)
    run_ref[...] = jnp.zeros_like(run_ref)

    def tile(j, masked):
        kb = k_ref[pl.ds(pl.multiple_of(j * tk, tk), tk), :]
        z = _dot(kb, qt)
        if masked:
            key, qry = _tile_iotas(tk, tq)
            z = jnp.where(key < qry, z, SB_NEG * LOG2E)
        sp = jnp.maximum(z, 0.0) + jnp.log(1.0 + jnp.exp2(-jnp.abs(z))) * LOG2E
        vb = vt_ref[j]
        run = run_ref[...]
        acc = acc_ref[...]
        for b in reversed(range(tk // CUM_BLOCK)):
            sl = slice(b * CUM_BLOCK, (b + 1) * CUM_BLOCK)
            sp_b = sp[sl]
            after = _dot(later, sp_b.astype(BF16)) + run
            w = jnp.exp2(z[sl] - sp_b - after).astype(BF16)
            acc = acc + _dot(vb[:, sl], w)
            run = run + jnp.sum(sp_b, axis=0, keepdims=True)
        run_ref[...] = run
        acc_ref[...] = acc

    _walk_causal(tile, qi)
    o_ref[...] = acc_ref[...].T.astype(o_ref.dtype)


def _softmax_step(z, vb, m_ref, l_ref, acc_ref, a):
    m_old = m_ref[a]
    m_new = jnp.maximum(m_old, jnp.max(z, axis=0, keepdims=True))
    alpha = jnp.exp2(m_old - m_new)
    p = jnp.exp2(z - m_new)
    l_ref[a] = alpha * l_ref[a] + jnp.sum(p, axis=0, keepdims=True)
    acc_ref[a] = alpha * acc_ref[a] + _dot(vb, p.astype(BF16))
    m_ref[a] = m_new


def _softmax_init(m_ref, l_ref, acc_ref):
    m_ref[...] = jnp.full_like(m_ref, NEG)
    l_ref[...] = jnp.zeros_like(l_ref)
    acc_ref[...] = jnp.zeros_like(acc_ref)


def _diff_kernel(qt_ref, k_ref, vt_ref, lam_ref, g_ref, o_ref, m_ref, l_ref, acc_ref, *, lam_init):
    qi = pl.program_id(1)
    tq = qt_ref.shape[1]
    tk = vt_ref.shape[2]
    qt = qt_ref[...]
    _softmax_init(m_ref, l_ref, acc_ref)

    def tile(j, masked):
        kb = k_ref[pl.ds(pl.multiple_of(j * tk, tk), tk), :]
        vb = vt_ref[j]
        for a in range(2):
            sl = slice(a * HEAD_DIM, (a + 1) * HEAD_DIM)
            z = _dot(kb[:, sl], qt[sl])
            if masked:
                key, qry = _tile_iotas(tk, tq)
                z = jnp.where(key <= qry, z, NEG)
            _softmax_step(z, vb, m_ref, l_ref, acc_ref, a)

    _walk_causal(tile, qi)
    lp = lam_ref[...]
    lam = (jnp.exp(jnp.sum(lp[0:1] * lp[1:2], axis=-1, keepdims=True))
           - jnp.exp(jnp.sum(lp[2:3] * lp[3:4], axis=-1, keepdims=True)) + lam_init)
    o = acc_ref[0] * (1.0 / l_ref[0]) - lam * (acc_ref[1] * (1.0 / l_ref[1]))
    o_ref[...] = (_rms(o.T, g_ref[...]) * (1.0 - lam_init)).astype(o_ref.dtype)


def _fox_kernel(qt_ref, k_ref, vt_ref, cs_ref, ct_ref, o_ref, m_ref, l_ref, acc_ref):
    h = pl.program_id(0)
    qi = pl.program_id(1)
    tq = qt_ref.shape[1]
    tk = vt_ref.shape[2]
    qt = qt_ref[...]
    ct_all = ct_ref[...]
    head = lax.broadcasted_iota(jnp.int32, ct_all.shape, 0)
    ct = jnp.sum(jnp.where(head == h, ct_all, 0.0), axis=0, keepdims=True)
    _softmax_init(m_ref, l_ref, acc_ref)

    def tile(j, masked):
        rows = pl.ds(pl.multiple_of(j * tk, tk), tk)
        cs = pltpu.repeat(cs_ref[rows, :], tq // LANES, 1)
        z = _dot(k_ref[rows, :], qt) + ct - cs
        if masked:
            key, qry = _tile_iotas(tk, tq)
            z = jnp.where(key <= qry, z, NEG)
        _softmax_step(z, vt_ref[j], m_ref, l_ref, acc_ref, 0)

    _walk_causal(tile, qi)
    o_ref[...] = (acc_ref[0] * (1.0 / l_ref[0])).T.astype(o_ref.dtype)


def _attention(kind, qt, k, vt, extra, layer):
    d, s = qt.shape
    nb, _, tk = vt.shape
    tq = ROW_TILE
    hw = 2 * HEAD_DIM if kind == 1 else HEAD_DIM
    n_heads = d // hw
    specs = [pl.BlockSpec((hw, tq), lambda h, i: (h, i)),
             pl.BlockSpec((s, hw), lambda h, i: (0, h)),
             pl.BlockSpec((nb, hw, tk), lambda h, i: (0, h, 0))]
    stats = [pltpu.VMEM((hw // HEAD_DIM, 1, tq), F32)] * 2 + [pltpu.VMEM((hw // HEAD_DIM, hw, tq), F32)]
    if kind == 0:
        kern, args, vmem = _sb_kernel, [], 40
        scratch = [pltpu.VMEM((HEAD_DIM, tq), F32), pltpu.VMEM((1, tq), F32)]
    elif kind == 1:
        lam_init = 0.8 - 0.6 * math.exp(-0.3 * layer)
        kern, args, vmem = functools.partial(_diff_kernel, lam_init=lam_init), list(extra), 48
        specs += [pl.BlockSpec((4, HEAD_DIM), lambda h, i: (0, 0)),
                  pl.BlockSpec((1, hw), lambda h, i: (0, 0))]
        scratch = stats
    else:
        kern, args, vmem = _fox_kernel, list(extra), 56
        specs += [pl.BlockSpec((None, s, LANES), lambda h, i: (h, 0, 0)),
                  pl.BlockSpec((d // HEAD_DIM, tq), lambda h, i: (0, i))]
        scratch = stats
    return pl.pallas_call(
        kern, grid=(n_heads, s // tq), in_specs=specs,
        out_specs=pl.BlockSpec((tq, hw), lambda h, i: (i, h)),
        out_shape=jax.ShapeDtypeStruct((s, d), BF16), scratch_shapes=scratch,
        compiler_params=_params(vmem, 2), name=f"attn{kind}")(qt, k, vt, *args)


def _prep_ffn(w_gate, w_up, w_down):
    d, f = w_gate.shape
    n = f // FFN_CHUNK
    wgu = jnp.stack([w_gate.reshape(d, n, FFN_CHUNK), w_up.reshape(d, n, FFN_CHUNK)], axis=2)
    return wgu.reshape(d, 2 * f).astype(BF16), w_down.astype(BF16)


def _rope_tables(s):
    inv = ROPE_THETA ** (-jnp.arange(ROT_HALF, dtype=F32) * 2.0 / ROT_DIM)
    ang = jnp.arange(s).astype(F32)[:, None] * inv[None, :]
    cos, sin = jnp.cos(ang), jnp.sin(ang)
    pad = HEAD_DIM - ROT_DIM
    zeros = jnp.zeros_like(sin)
    cos_f = jnp.concatenate([cos, cos, jnp.ones((s, pad), F32)], axis=1)
    sin_a = jnp.concatenate([-sin, zeros, jnp.zeros((s, pad), F32)], axis=1)
    sin_b = jnp.concatenate([zeros, sin, jnp.zeros((s, pad), F32)], axis=1)
    return cos.T, sin.T, cos_f, sin_a, sin_b


def kernel(x, norm_g, final_g, ffn_w_gate, ffn_w_up, ffn_w_down, w_qkv, w_o,
           diff_lambda, diff_subln_g, fox_w_f, fox_b_f):
    b, s, d = x.shape
    assert b == 1 and s % ROW_TILE == 0 and d % (2 * HEAD_DIM) == 0
    depth = w_qkv.shape[0]
    n_heads = d // HEAD_DIM
    xs = x.reshape(s, d)
    for i in range(depth):
        kind, j = i % N_MIXERS, i // N_MIXERS
        xs = _ffn(xs, norm_g[i, 0][None], *_prep_ffn(ffn_w_gate[i, 0], ffn_w_up[i, 0], ffn_w_down[i, 0]))
        wq, wk, wv = w_qkv[i][:, :d], w_qkv[i][:, d:2 * d], w_qkv[i][:, 2 * d:]
        rope = fox = None
        if kind == 1:
            rope = _rope_tables(s)
        if kind == 2:
            wf = jnp.zeros((d, LANES), F32).at[:, :n_heads].set(fox_w_f[j]).astype(BF16)
            bf = jnp.zeros((1, LANES), F32).at[0, :n_heads].set(fox_b_f[j])
            fox = (wf, bf)
        outs = _qkv(xs, norm_g[i, 1][None], wq.T.astype(BF16), wk.astype(BF16), wv.T.astype(BF16),
                    kind, rope=rope, fox=fox)
        extra = ()
        if kind == 1:
            extra = (diff_lambda[j], diff_subln_g[j][None])
        if kind == 2:
            extra = (outs[3], outs[4])
        mix = _attention(kind, outs[0], outs[1], outs[2], extra, i)
        xs = _ffn(xs, norm_g[i, 2][None], *_prep_ffn(ffn_w_gate[i, 1], ffn_w_up[i, 1], ffn_w_down[i, 1]),
                  proj=(mix, w_o[i].astype(BF16)),
                  final_g=final_g[None] if i == depth - 1 else None)
    return xs.reshape(b, s, d)
```

```python
import functools
import math

import jax
import jax.numpy as jnp
from jax import lax
from jax.experimental import pallas as pl
from jax.experimental.pallas import tpu as pltpu

F32 = jnp.float32
BF16 = jnp.bfloat16

HEAD_DIM = 128
N_MIXERS = 3
ROPE_THETA = 500000.0
ROT_DIM = HEAD_DIM // 4
ROT_HALF = ROT_DIM // 2
EPS = 1e-6
NEG = -1e30
SB_NEG = -1e4
LOG2E = 1.4426950408889634
Q_SCALE = HEAD_DIM ** -0.5 * LOG2E

LANES = 128
MXU_DIM = 256
VMEM_BYTES = 64 * 1024 * 1024

ROW_TILE = 512
FFN_CHUNK = 256
CUM_BLOCK = MXU_DIM


def _dot(a, b):
    return jnp.dot(a, b, preferred_element_type=F32)


def _dot_nt(a, b):
    return lax.dot_general(a, b, (((1,), (1,)), ((), ())), preferred_element_type=F32)


def _rms(x, g):
    return x * lax.rsqrt(jnp.mean(x * x, axis=-1, keepdims=True) + EPS) * g


def _params(vmem_mib, n_axes):
    return pltpu.CompilerParams(
        dimension_semantics=("arbitrary",) * n_axes,
        vmem_limit_bytes=vmem_mib * 1024 * 1024)


def _resident(shape, index_map):
    return pl.BlockSpec(shape, index_map, pipeline_mode=pl.Buffered(1))


def _ffn_kernel(*refs, has_proj, has_final, n_chunks):
    it = iter(refs)
    x_ref = next(it)
    if has_proj:
        mix_ref, wo_ref = next(it), next(it)
    g_ref, wgu_ref, wd_ref = next(it), next(it), next(it)
    if has_final:
        fg_ref = next(it)
    o_ref, a_ref = next(it), next(it)

    x = x_ref[...]
    if has_proj:
        x = x + _dot(mix_ref[...], wo_ref[...])
    h = _rms(x, g_ref[...]).astype(BF16)
    c2 = 2 * FFN_CHUNK
    for c in range(n_chunks):
        gu = _dot(h, wgu_ref[:, c * c2:(c + 1) * c2])
        gate, up = gu[:, :FFN_CHUNK], gu[:, FFN_CHUNK:]
        a_ref[:, c * FFN_CHUNK:(c + 1) * FFN_CHUNK] = (gate * jax.nn.sigmoid(gate) * up).astype(BF16)
    y = x + 0.5 * _dot(a_ref[...], wd_ref[...])
    if has_final:
        y = _rms(y, fg_ref[...])
    o_ref[...] = y


def _ffn(x, g, wgu, wd, proj=None, final_g=None):
    s, d = x.shape
    f = wd.shape[0]
    tm = ROW_TILE
    row = lambda i: (i, 0)
    const = lambda i: (0, 0)
    args, specs = [x], [pl.BlockSpec((tm, d), row)]
    if proj is not None:
        mix, wo = proj
        args += [mix, wo]
        specs += [pl.BlockSpec((tm, d), row), _resident((d, d), const)]
    args += [g, wgu, wd]
    specs += [_resident((1, d), const), _resident((d, 2 * f), const), _resident((f, d), const)]
    if final_g is not None:
        args.append(final_g)
        specs.append(_resident((1, d), const))
    kern = functools.partial(_ffn_kernel, has_proj=proj is not None, has_final=final_g is not None,
                             n_chunks=f // FFN_CHUNK)
    return pl.pallas_call(
        kern, grid=(s // tm,), in_specs=specs, out_specs=pl.BlockSpec((tm, d), row),
        out_shape=jax.ShapeDtypeStruct((s, d), F32),
        scratch_shapes=[pltpu.VMEM((tm, f), BF16)],
        compiler_params=_params(48, 1), name="ffn")(*args)


def _qkv_kernel(*refs, kind, n_heads):
    it = iter(refs)
    x_ref, g_ref, wqt_ref, wk_ref, wvt_ref = (next(it) for _ in range(5))
    if kind == 1:
        cost_ref, sint_ref, cosf_ref, sina_ref, sinb_ref = (next(it) for _ in range(5))
    if kind == 2:
        wf_ref, bf_ref = next(it), next(it)
    qt_ref, k_ref, vt_ref = next(it), next(it), next(it)
    if kind == 2:
        cs_ref, ct_ref, carry_ref = next(it), next(it), next(it)

    h = _rms(x_ref[...], g_ref[...]).astype(BF16)
    qt = _dot_nt(wqt_ref[...], h) * Q_SCALE
    k = _dot(h, wk_ref[...])
    vt_ref[...] = _dot_nt(wvt_ref[...], h).astype(BF16)

    if kind != 1:
        qt_ref[...] = qt.astype(BF16)
        k_ref[...] = k.astype(BF16)
    else:
        cos_t, sin_t = cost_ref[...], sint_ref[...]
        cos_f, sin_a, sin_b = cosf_ref[...], sina_ref[...], sinb_ref[...]
        for hh in range(n_heads):
            r0 = hh * HEAD_DIM
            x1, x2 = qt[r0:r0 + ROT_HALF], qt[r0 + ROT_HALF:r0 + ROT_DIM]
            qt_ref[r0:r0 + ROT_HALF, :] = (x1 * cos_t - x2 * sin_t).astype(BF16)
            qt_ref[r0 + ROT_HALF:r0 + ROT_DIM, :] = (x2 * cos_t + x1 * sin_t).astype(BF16)
            qt_ref[r0 + ROT_DIM:r0 + HEAD_DIM, :] = qt[r0 + ROT_DIM:r0 + HEAD_DIM].astype(BF16)
            kh = k[:, r0:r0 + HEAD_DIM]
            rot = (kh * cos_f + pltpu.roll(kh, HEAD_DIM - ROT_HALF, 1) * sin_a
                   + pltpu.roll(kh, ROT_HALF, 1) * sin_b)
            k_ref[:, r0:r0 + HEAD_DIM] = rot.astype(BF16)

    if kind == 2:
        tm = h.shape[0]

        @pl.when(pl.program_id(0) == 0)
        def _():
            carry_ref[...] = jnp.zeros_like(carry_ref)

        fl = _dot(h, wf_ref[...]) + bf_ref[...]
        lf = jnp.minimum(fl, 0.0) - jnp.log(1.0 + jnp.exp(-jnp.abs(fl)))
        r = lax.broadcasted_iota(jnp.int32, (tm, tm), 0)
        c = lax.broadcasted_iota(jnp.int32, (tm, tm), 1)
        tri = jnp.where(c <= r, 1.0, 0.0).astype(BF16)
        p0 = lf.astype(BF16)
        r1 = lf - p0.astype(F32)
        p1 = r1.astype(BF16)
        p2 = (r1 - p1.astype(F32)).astype(BF16)
        csum = _dot(tri, p0) + _dot(tri, p1) + _dot(tri, p2) + carry_ref[...]
        carry_ref[...] = csum[tm - 1:tm, :]
        c2 = csum * LOG2E
        ct_ref[...] = c2.T[:n_heads, :]
        for hh in range(n_heads):
            cs_ref[hh] = jnp.broadcast_to(c2[:, hh:hh + 1], (tm, LANES))


def _qkv(x, g, wqt, wk, wvt, kind, rope=None, fox=None):
    s, d = x.shape
    tm = ROW_TILE
    nb = s // tm
    n_heads = d // HEAD_DIM
    row = lambda i: (i, 0)
    col = lambda i: (0, i)
    const = lambda i: (0, 0)
    args = [x, g, wqt, wk, wvt]
    specs = [pl.BlockSpec((tm, d), row), _resident((1, d), const),
             _resident((d, d), const), _resident((d, d), const), _resident((d, d), const)]
    out_shape = [jax.ShapeDtypeStruct((d, s), BF16), jax.ShapeDtypeStruct((s, d), BF16),
                 jax.ShapeDtypeStruct((nb, d, tm), BF16)]
    out_specs = [pl.BlockSpec((d, tm), col), pl.BlockSpec((tm, d), row),
                 pl.BlockSpec((None, d, tm), lambda i: (i, 0, 0))]
    scratch = []
    if kind == 1:
        args += list(rope)
        specs += [pl.BlockSpec((ROT_HALF, tm), col)] * 2 + [pl.BlockSpec((tm, HEAD_DIM), row)] * 3
    if kind == 2:
        args += list(fox)
        specs += [_resident((d, LANES), const), _resident((1, LANES), const)]
        out_shape += [jax.ShapeDtypeStruct((n_heads, s, LANES), F32),
                      jax.ShapeDtypeStruct((n_heads, s), F32)]
        out_specs += [pl.BlockSpec((n_heads, tm, LANES), lambda i: (0, i, 0)),
                      pl.BlockSpec((n_heads, tm), col)]
        scratch = [pltpu.VMEM((1, LANES), F32)]
    kern = functools.partial(_qkv_kernel, kind=kind, n_heads=n_heads)
    return pl.pallas_call(
        kern, grid=(nb,), in_specs=specs, out_specs=out_specs, out_shape=out_shape,
        scratch_shapes=scratch, compiler_params=_params(40, 1), name=f"qkv{kind}")(*args)


def _tile_iotas(tk, tq):
    return (lax.broadcasted_iota(jnp.int32, (tk, tq), 0),
            lax.broadcasted_iota(jnp.int32, (tk, tq), 1))


def _run_steps(step, n_tiles, set0, set1, finish):
    def pair(i, carry):
        step(2 * i, set0, set1)
        step(2 * i + 1, set1, set0)
        return carry

    lax.fori_loop(0, lax.shift_right_logical(n_tiles, 1), pair, 0)
    odd = lax.bitwise_and(n_tiles, 1) == 1

    @pl.when(odd)
    def _():
        step(n_tiles - 1, set0, set1)
        finish(set0)

    @pl.when(jnp.logical_not(odd))
    def _():
        finish(set1)


def _sb_kernel(qt_ref, k_ref, vt_ref, o_ref, acc_ref, run_ref, *bufs):
    qi = pl.program_id(1)
    tq = qt_ref.shape[1]
    tk = vt_ref.shape[2]
    n_sub = tk // CUM_BLOCK
    qt = qt_ref[...]
    d_i, c_i = _tile_iotas(CUM_BLOCK, CUM_BLOCK)
    later = jnp.where(c_i > d_i, 1.0, 0.0).astype(BF16)
    set0, set1 = bufs[:5], bufs[5:]
    acc_ref[...] = jnp.zeros_like(acc_ref)
    run_ref[...] = jnp.zeros_like(run_ref)
    set1[4][...] = jnp.zeros_like(set1[4])

    def put_scores(j, masked, z_ref):
        kb = k_ref[pl.ds(pl.multiple_of(j * tk, tk), tk), :]
        z = _dot(kb, qt)
        if masked:
            key, qry = _tile_iotas(tk, tq)
            z = jnp.where(key < qry, z, SB_NEG * LOG2E)
        z_ref[...] = z

    def softplus_pass(bufset):
        z_ref, d_ref, sp_ref, cs_ref, _ = bufset
        for b in range(n_sub):
            sl = slice(b * CUM_BLOCK, (b + 1) * CUM_BLOCK)
            z = z_ref[sl, :]
            sp = jnp.maximum(z, 0.0) + jnp.log(1.0 + jnp.exp2(-jnp.abs(z))) * LOG2E
            d_ref[sl, :] = z - sp
            sp_ref[sl, :] = sp.astype(BF16)
            cs_ref[b] = jnp.sum(sp, axis=0, keepdims=True)

    def value_update(j, w_ref):
        acc_ref[...] += _dot(vt_ref[j], w_ref[...])

    def step(n, cur, oth):
        z_c, d_c, sp_c, cs_c, w_c = cur
        j = qi - n
        after = [_dot(later, sp_c[b * CUM_BLOCK:(b + 1) * CUM_BLOCK, :]) for b in range(n_sub)]
        value_update(jnp.minimum(j + 1, qi), oth[4])
        put_scores(jnp.maximum(j - 2, 0), False, z_c)
        run = run_ref[...]
        for b in reversed(range(n_sub)):
            sl = slice(b * CUM_BLOCK, (b + 1) * CUM_BLOCK)
            w_c[sl, :] = jnp.exp2(d_c[sl, :] - (after[b] + run)).astype(BF16)
            run = run + cs_c[b]
        run_ref[...] = run
        softplus_pass(oth)

    put_scores(qi, True, set0[0])
    softplus_pass(set0)
    put_scores(jnp.maximum(qi - 1, 0), False, set1[0])
    _run_steps(step, qi + 1, set0, set1, lambda last: value_update(0, last[4]))
    o_ref[...] = acc_ref[...].T.astype(o_ref.dtype)


def _softmax_pipeline(qi, score, vt_ref, bufs, m_ref, l_ref, acc_ref):
    (z0, p0, a0), (z1, p1, a1) = bufs
    n_streams = m_ref.shape[0]
    m_ref[...] = jnp.full_like(m_ref, NEG)
    l_ref[...] = jnp.zeros_like(l_ref)
    acc_ref[...] = jnp.zeros_like(acc_ref)
    p1[...] = jnp.zeros_like(p1)
    a1[...] = jnp.ones_like(a1)

    def put_scores(j, masked, z_ref):
        for a, z in enumerate(score(j, masked)):
            z_ref[a] = z

    def value_update(j, p_ref, a_ref):
        vb = vt_ref[j]
        for a in range(n_streams):
            acc_ref[a] = a_ref[a] * acc_ref[a] + _dot(vb, p_ref[a])

    def softmax_pass(z_ref, p_ref, a_ref):
        for a in range(n_streams):
            z = z_ref[a]
            m_old = m_ref[a]
            m_new = jnp.maximum(m_old, jnp.max(z, axis=0, keepdims=True))
            alpha = jnp.exp2(m_old - m_new)
            p = jnp.exp2(z - m_new)
            l_ref[a] = alpha * l_ref[a] + jnp.sum(p, axis=0, keepdims=True)
            m_ref[a] = m_new
            a_ref[a] = alpha
            p_ref[a] = p.astype(BF16)

    def step(n, cur, nxt):
        (zc, pc, ac), (zn, pn, an) = cur, nxt
        j = qi - n
        value_update(jnp.minimum(j + 1, qi), pn, an)
        put_scores(jnp.maximum(j - 1, 0), False, zn)
        softmax_pass(zc, pc, ac)

    put_scores(qi, True, z0)
    _run_steps(step, qi + 1, (z0, p0, a0), (z1, p1, a1),
               lambda last: value_update(0, last[1], last[2]))


def _diff_kernel(qt_ref, k_ref, vt_ref, lam_ref, g_ref, o_ref, m_ref, l_ref, acc_ref,
                 z0, p0, a0, z1, p1, a1, *, lam_init):
    qi = pl.program_id(1)
    tq = qt_ref.shape[1]
    tk = vt_ref.shape[2]
    qt = qt_ref[...]

    def scores(j, masked):
        kb = k_ref[pl.ds(pl.multiple_of(j * tk, tk), tk), :]
        zs = []
        for a in range(2):
            sl = slice(a * HEAD_DIM, (a + 1) * HEAD_DIM)
            z = _dot(kb[:, sl], qt[sl])
            if masked:
                key, qry = _tile_iotas(tk, tq)
                z = jnp.where(key <= qry, z, NEG)
            zs.append(z)
        return zs

    _softmax_pipeline(qi, scores, vt_ref, ((z0, p0, a0), (z1, p1, a1)), m_ref, l_ref, acc_ref)
    lp = lam_ref[...]
    lam = (jnp.exp(jnp.sum(lp[0:1] * lp[1:2], axis=-1, keepdims=True))
           - jnp.exp(jnp.sum(lp[2:3] * lp[3:4], axis=-1, keepdims=True)) + lam_init)
    o = acc_ref[0] * (1.0 / l_ref[0]) - lam * (acc_ref[1] * (1.0 / l_ref[1]))
    o_ref[...] = (_rms(o.T, g_ref[...]) * (1.0 - lam_init)).astype(o_ref.dtype)


def _fox_kernel(qt_ref, k_ref, vt_ref, cs_ref, ct_ref, o_ref, m_ref, l_ref, acc_ref,
                z0, p0, a0, z1, p1, a1):
    h = pl.program_id(0)
    qi = pl.program_id(1)
    tq = qt_ref.shape[1]
    tk = vt_ref.shape[2]
    qt = qt_ref[...]
    ct_all = ct_ref[...]
    head = lax.broadcasted_iota(jnp.int32, ct_all.shape, 0)
    ct = jnp.sum(jnp.where(head == h, ct_all, 0.0), axis=0, keepdims=True)

    def scores(j, masked):
        rows = pl.ds(pl.multiple_of(j * tk, tk), tk)
        cs = pltpu.repeat(cs_ref[rows, :], tq // LANES, 1)
        z = _dot(k_ref[rows, :], qt) + ct - cs
        if masked:
            key, qry = _tile_iotas(tk, tq)
            z = jnp.where(key <= qry, z, NEG)
        return [z]

    _softmax_pipeline(qi, scores, vt_ref, ((z0, p0, a0), (z1, p1, a1)), m_ref, l_ref, acc_ref)
    o_ref[...] = (acc_ref[0] * (1.0 / l_ref[0])).T.astype(o_ref.dtype)


def _attention(kind, qt, k, vt, extra, layer):
    d, s = qt.shape
    nb, _, tk = vt.shape
    tq = ROW_TILE
    hw = 2 * HEAD_DIM if kind == 1 else HEAD_DIM
    n_heads = d // hw
    specs = [pl.BlockSpec((hw, tq), lambda h, i: (h, i)),
             pl.BlockSpec((s, hw), lambda h, i: (0, h)),
             pl.BlockSpec((nb, hw, tk), lambda h, i: (0, h, 0))]
    ns = hw // HEAD_DIM
    stats = [pltpu.VMEM((ns, 1, tq), F32)] * 2 + [pltpu.VMEM((ns, hw, tq), F32)]
    stats += [pltpu.VMEM((ns, tk, tq), F32), pltpu.VMEM((ns, tk, tq), BF16),
              pltpu.VMEM((ns, 1, tq), F32)] * 2
    if kind == 0:
        kern, args, vmem = _sb_kernel, [], 40
        scratch = [pltpu.VMEM((HEAD_DIM, tq), F32), pltpu.VMEM((1, tq), F32)]
        scratch += [pltpu.VMEM((tk, tq), F32), pltpu.VMEM((tk, tq), F32), pltpu.VMEM((tk, tq), BF16),
                    pltpu.VMEM((tk // CUM_BLOCK, 1, tq), F32), pltpu.VMEM((tk, tq), BF16)] * 2
    elif kind == 1:
        lam_init = 0.8 - 0.6 * math.exp(-0.3 * layer)
        kern, args, vmem = functools.partial(_diff_kernel, lam_init=lam_init), list(extra), 48
        specs += [pl.BlockSpec((4, HEAD_DIM), lambda h, i: (0, 0)),
                  pl.BlockSpec((1, hw), lambda h, i: (0, 0))]
        scratch = stats
    else:
        kern, args, vmem = _fox_kernel, list(extra), 56
        specs += [pl.BlockSpec((None, s, LANES), lambda h, i: (h, 0, 0)),
                  pl.BlockSpec((d // HEAD_DIM, tq), lambda h, i: (0, i))]
        scratch = stats
    return pl.pallas_call(
        kern, grid=(n_heads, s // tq), in_specs=specs,
        out_specs=pl.BlockSpec((tq, hw), lambda h, i: (i, h)),
        out_shape=jax.ShapeDtypeStruct((s, d), BF16), scratch_shapes=scratch,
        compiler_params=_params(vmem, 2), name=f"attn{kind}")(qt, k, vt, *args)


def _prep_ffn(w_gate, w_up, w_down):
    d, f = w_gate.shape
    n = f // FFN_CHUNK
    wgu = jnp.stack([w_gate.reshape(d, n, FFN_CHUNK), w_up.reshape(d, n, FFN_CHUNK)], axis=2)
    return wgu.reshape(d, 2 * f).astype(BF16), w_down.astype(BF16)


def _rope_tables(s):
    inv = ROPE_THETA ** (-jnp.arange(ROT_HALF, dtype=F32) * 2.0 / ROT_DIM)
    ang = jnp.arange(s).astype(F32)[:, None] * inv[None, :]
    cos, sin = jnp.cos(ang), jnp.sin(ang)
    pad = HEAD_DIM - ROT_DIM
    zeros = jnp.zeros_like(sin)
    cos_f = jnp.concatenate([cos, cos, jnp.ones((s, pad), F32)], axis=1)
    sin_a = jnp.concatenate([-sin, zeros, jnp.zeros((s, pad), F32)], axis=1)
    sin_b = jnp.concatenate([zeros, sin, jnp.zeros((s, pad), F32)], axis=1)
    return cos.T, sin.T, cos_f, sin_a, sin_b


def kernel(x, norm_g, final_g, ffn_w_gate, ffn_w_up, ffn_w_down, w_qkv, w_o,
           diff_lambda, diff_subln_g, fox_w_f, fox_b_f):
    b, s, d = x.shape
    assert b == 1 and s % ROW_TILE == 0 and d % (2 * HEAD_DIM) == 0
    depth = w_qkv.shape[0]
    n_heads = d // HEAD_DIM
    xs = x.reshape(s, d)
    for i in range(depth):
        kind, j = i % N_MIXERS, i // N_MIXERS
        xs = _ffn(xs, norm_g[i, 0][None], *_prep_ffn(ffn_w_gate[i, 0], ffn_w_up[i, 0], ffn_w_down[i, 0]))
        wq, wk, wv = w_qkv[i][:, :d], w_qkv[i][:, d:2 * d], w_qkv[i][:, 2 * d:]
        rope = fox = None
        if kind == 1:
            rope = _rope_tables(s)
        if kind == 2:
            wf = jnp.zeros((d, LANES), F32).at[:, :n_heads].set(fox_w_f[j]).astype(BF16)
            bf = jnp.zeros((1, LANES), F32).at[0, :n_heads].set(fox_b_f[j])
            fox = (wf, bf)
        outs = _qkv(xs, norm_g[i, 1][None], wq.T.astype(BF16), wk.astype(BF16), wv.T.astype(BF16),
                    kind, rope=rope, fox=fox)
        extra = ()
        if kind == 1:
            extra = (diff_lambda[j], diff_subln_g[j][None])
        if kind == 2:
            extra = (outs[3], outs[4])
        mix = _attention(kind, outs[0], outs[1], outs[2], extra, i)
        xs = _ffn(xs, norm_g[i, 2][None], *_prep_ffn(ffn_w_gate[i, 1], ffn_w_up[i, 1], ffn_w_down[i, 1]),
                  proj=(mix, w_o[i].astype(BF16)),
                  final_g=final_g[None] if i == depth - 1 else None)
    return xs.reshape(b, s, d)
```

```python
import functools
import math

import jax
import jax.numpy as jnp
from jax import lax
from jax.experimental import pallas as pl
from jax.experimental.pallas import tpu as pltpu

F32 = jnp.float32
BF16 = jnp.bfloat16

HEAD_DIM = 128
N_MIXERS = 3
ROPE_THETA = 500000.0
ROT_DIM = HEAD_DIM // 4
ROT_HALF = ROT_DIM // 2
EPS = 1e-6
NEG = -1e30
SB_NEG = -1e4
LOG2E = 1.4426950408889634
Q_SCALE = HEAD_DIM ** -0.5 * LOG2E

LANES = 128
MXU_DIM = 256
VMEM_BYTES = 64 * 1024 * 1024

ROW_TILE = 512
FFN_CHUNK = 256
CUM_BLOCK = MXU_DIM
DEAD_LOG2 = 160.0
BOUND_SLACK = 1.0


def _dot(a, b):
    return jnp.dot(a, b, preferred_element_type=F32)


def _dot_nt(a, b):
    return lax.dot_general(a, b, (((1,), (1,)), ((), ())), preferred_element_type=F32)


def _rms(x, g):
    return x * lax.rsqrt(jnp.mean(x * x, axis=-1, keepdims=True) + EPS) * g


def _params(vmem_mib, n_axes):
    return pltpu.CompilerParams(
        dimension_semantics=("arbitrary",) * n_axes,
        vmem_limit_bytes=vmem_mib * 1024 * 1024)


def _resident(shape, index_map):
    return pl.BlockSpec(shape, index_map, pipeline_mode=pl.Buffered(1))


def _ffn_kernel(*refs, has_proj, has_final, n_chunks):
    it = iter(refs)
    x_ref = next(it)
    if has_proj:
        mix_ref, wo_ref = next(it), next(it)
    g_ref, wg_ref, wu_ref, wd_ref = next(it), next(it), next(it), next(it)
    if has_final:
        fg_ref = next(it)
    o_ref, a_ref = next(it), next(it)

    x = x_ref[...]
    if has_proj:
        x = x + _dot(mix_ref[...], wo_ref[...])
    h = _rms(x, g_ref[...]).astype(BF16)
    for c in range(n_chunks):
        cols = slice(c * FFN_CHUNK, (c + 1) * FFN_CHUNK)
        gate = _dot(h, wg_ref[:, cols])
        up = _dot(h, wu_ref[:, cols])
        a_ref[:, cols] = (gate * jax.nn.sigmoid(gate) * up).astype(BF16)
    y = x + 0.5 * _dot(a_ref[...], wd_ref[...])
    if has_final:
        y = _rms(y, fg_ref[...])
    o_ref[...] = y


def _ffn(x, g, wg, wu, wd, proj=None, final_g=None):
    s, d = x.shape
    f = wd.shape[0]
    tm = ROW_TILE
    row = lambda i: (i, 0)
    const = lambda i: (0, 0)
    args, specs = [x], [pl.BlockSpec((tm, d), row)]
    if proj is not None:
        mix, wo = proj
        args += [mix, wo]
        specs += [pl.BlockSpec((tm, d), row), _resident((d, d), const)]
    args += [g, wg, wu, wd]
    specs += [_resident((1, d), const), _resident((d, f), const), _resident((d, f), const),
              _resident((f, d), const)]
    if final_g is not None:
        args.append(final_g)
        specs.append(_resident((1, d), const))
    kern = functools.partial(_ffn_kernel, has_proj=proj is not None, has_final=final_g is not None,
                             n_chunks=f // FFN_CHUNK)
    return pl.pallas_call(
        kern, grid=(s // tm,), in_specs=specs, out_specs=pl.BlockSpec((tm, d), row),
        out_shape=jax.ShapeDtypeStruct((s, d), F32),
        scratch_shapes=[pltpu.VMEM((tm, f), BF16)],
        compiler_params=_params(48, 1), name="ffn")(*args)


def _qkv_kernel(*refs, kind, n_heads):
    it = iter(refs)
    x_ref, g_ref, wqt_ref, wk_ref, wvt_ref = (next(it) for _ in range(5))
    if kind == 1:
        cost_ref, sint_ref, cosf_ref, sina_ref, sinb_ref = (next(it) for _ in range(5))
    if kind == 2:
        wf_ref, bf_ref = next(it), next(it)
    qt_ref, k_ref, vt_ref = next(it), next(it), next(it)
    if kind == 2:
        cs_ref, ct_ref, kmax_ref, carry_ref, kcarry_ref = (next(it) for _ in range(5))

    h = _rms(x_ref[...], g_ref[...]).astype(BF16)
    qt = _dot_nt(wqt_ref[...], h) * Q_SCALE
    k = _dot(h, wk_ref[...])
    vt_ref[...] = _dot_nt(wvt_ref[...], h).astype(BF16)

    if kind != 1:
        qt_ref[...] = qt.astype(BF16)
        k_ref[...] = k.astype(BF16)
    else:
        cos_t, sin_t = cost_ref[...], sint_ref[...]
        cos_f, sin_a, sin_b = cosf_ref[...], sina_ref[...], sinb_ref[...]
        for hh in range(n_heads):
            r0 = hh * HEAD_DIM
            x1, x2 = qt[r0:r0 + ROT_HALF], qt[r0 + ROT_HALF:r0 + ROT_DIM]
            qt_ref[r0:r0 + ROT_HALF, :] = (x1 * cos_t - x2 * sin_t).astype(BF16)
            qt_ref[r0 + ROT_HALF:r0 + ROT_DIM, :] = (x2 * cos_t + x1 * sin_t).astype(BF16)
            qt_ref[r0 + ROT_DIM:r0 + HEAD_DIM, :] = qt[r0 + ROT_DIM:r0 + HEAD_DIM].astype(BF16)
            kh = k[:, r0:r0 + HEAD_DIM]
            rot = (kh * cos_f + pltpu.roll(kh, HEAD_DIM - ROT_HALF, 1) * sin_a
                   + pltpu.roll(kh, ROT_HALF, 1) * sin_b)
            k_ref[:, r0:r0 + HEAD_DIM] = rot.astype(BF16)

    if kind == 2:
        tm = h.shape[0]

        @pl.when(pl.program_id(0) == 0)
        def _():
            carry_ref[...] = jnp.zeros_like(carry_ref)
            kcarry_ref[...] = jnp.zeros_like(kcarry_ref)

        kb = k.astype(BF16).astype(F32)
        lane = lax.broadcasted_iota(jnp.int32, (1, LANES), 1)
        kmax = kcarry_ref[...]
        for hh in range(n_heads):
            kh = kb[:, hh * HEAD_DIM:(hh + 1) * HEAD_DIM]
            n2 = jnp.max(jnp.sum(kh * kh, axis=1, keepdims=True), axis=0, keepdims=True)
            kmax = jnp.where(lane == hh, jnp.maximum(kmax, jnp.sqrt(n2)), kmax)
        kcarry_ref[...] = kmax
        kmax_ref[...] = kmax

        fl = _dot(h, wf_ref[...]) + bf_ref[...]
        lf = jnp.minimum(fl, 0.0) - jnp.log(1.0 + jnp.exp(-jnp.abs(fl)))
        r = lax.broadcasted_iota(jnp.int32, (tm, tm), 0)
        c = lax.broadcasted_iota(jnp.int32, (tm, tm), 1)
        tri = jnp.where(c <= r, 1.0, 0.0).astype(BF16)
        p0 = lf.astype(BF16)
        r1 = lf - p0.astype(F32)
        p1 = r1.astype(BF16)
        p2 = (r1 - p1.astype(F32)).astype(BF16)
        csum = _dot(tri, p0) + _dot(tri, p1) + _dot(tri, p2) + carry_ref[...]
        carry_ref[...] = csum[tm - 1:tm, :]
        c2 = csum * LOG2E
        ct_ref[...] = c2.T[:n_heads, :]
        for hh in range(n_heads):
            cs_ref[hh] = jnp.broadcast_to(c2[:, hh:hh + 1], (tm, LANES))


def _qkv(x, g, wqt, wk, wvt, kind, rope=None, fox=None):
    s, d = x.shape
    tm = ROW_TILE
    nb = s // tm
    n_heads = d // HEAD_DIM
    row = lambda i: (i, 0)
    col = lambda i: (0, i)
    const = lambda i: (0, 0)
    args = [x, g, wqt, wk, wvt]
    specs = [pl.BlockSpec((tm, d), row), _resident((1, d), const),
             _resident((d, d), const), _resident((d, d), const), _resident((d, d), const)]
    out_shape = [jax.ShapeDtypeStruct((d, s), BF16), jax.ShapeDtypeStruct((s, d), BF16),
                 jax.ShapeDtypeStruct((nb, d, tm), BF16)]
    out_specs = [pl.BlockSpec((d, tm), col), pl.BlockSpec((tm, d), row),
                 pl.BlockSpec((None, d, tm), lambda i: (i, 0, 0))]
    scratch = []
    if kind == 1:
        args += list(rope)
        specs += [pl.BlockSpec((ROT_HALF, tm), col)] * 2 + [pl.BlockSpec((tm, HEAD_DIM), row)] * 3
    if kind == 2:
        args += list(fox)
        specs += [_resident((d, LANES), const), _resident((1, LANES), const)]
        out_shape += [jax.ShapeDtypeStruct((n_heads, s, LANES), F32),
                      jax.ShapeDtypeStruct((n_heads, s), F32),
                      jax.ShapeDtypeStruct((nb, 1, LANES), F32)]
        out_specs += [pl.BlockSpec((n_heads, tm, LANES), lambda i: (0, i, 0)),
                      pl.BlockSpec((n_heads, tm), col),
                      pl.BlockSpec((None, 1, LANES), lambda i: (i, 0, 0))]
        scratch = [pltpu.VMEM((1, LANES), F32)] * 2
    kern = functools.partial(_qkv_kernel, kind=kind, n_heads=n_heads)
    return pl.pallas_call(
        kern, grid=(nb,), in_specs=specs, out_specs=out_specs, out_shape=out_shape,
        scratch_shapes=scratch, compiler_params=_params(40, 1), name=f"qkv{kind}")(*args)


def _tile_iotas(tk, tq):
    return (lax.broadcasted_iota(jnp.int32, (tk, tq), 0),
            lax.broadcasted_iota(jnp.int32, (tk, tq), 1))


def _run_steps(step, n_tiles, set0, set1, finish):
    def pair(i, carry):
        step(2 * i, set0, set1)
        step(2 * i + 1, set1, set0, scores_first=True)
        return carry

    lax.fori_loop(0, lax.shift_right_logical(n_tiles, 1), pair, 0)
    odd = lax.bitwise_and(n_tiles, 1) == 1

    @pl.when(odd)
    def _():
        step(n_tiles - 1, set0, set1)
        finish(set0)

    @pl.when(jnp.logical_not(odd))
    def _():
        finish(set1)


def _sb_kernel(qt_ref, k_ref, vt_ref, o_ref, acc_ref, run_ref):
    qi = pl.program_id(1)
    tq = qt_ref.shape[1]
    tk = vt_ref.shape[2]
    qt = qt_ref[...]
    d_i, c_i = _tile_iotas(CUM_BLOCK, CUM_BLOCK)
    later = jnp.where(c_i > d_i, 1.0, 0.0).astype(BF16)
    acc_ref[...] = jnp.zeros_like(acc_ref)
    run_ref[...] = jnp.zeros_like(run_ref)

    def tile(j, masked):
        kb = k_ref[pl.ds(pl.multiple_of(j * tk, tk), tk), :]
        z = _dot(kb, qt)
        if masked:
            key, qry = _tile_iotas(tk, tq)
            z = jnp.where(key < qry, z, SB_NEG * LOG2E)
        sp = jnp.maximum(z, 0.0) + jnp.log(1.0 + jnp.exp2(-jnp.abs(z))) * LOG2E
        vb = vt_ref[j]
        run = run_ref[...]
        acc = acc_ref[...]
        for b in reversed(range(tk // CUM_BLOCK)):
            sl = slice(b * CUM_BLOCK, (b + 1) * CUM_BLOCK)
            sp_b = sp[sl]
            after = _dot(later, sp_b.astype(BF16)) + run
            w = jnp.exp2(z[sl] - sp_b - after).astype(BF16)
            acc = acc + _dot(vb[:, sl], w)
            run = run + jnp.sum(sp_b, axis=0, keepdims=True)
        run_ref[...] = run
        acc_ref[...] = acc
        return (jnp.min(run) < DEAD_LOG2).astype(jnp.int32)

    def body(carry):
        n, _ = carry
        return n + 1, tile(qi - n, False)

    lax.while_loop(lambda c: (c[0] <= qi) & (c[1] > 0), body, (jnp.int32(1), tile(qi, True)))
    o_ref[...] = acc_ref[...].T.astype(o_ref.dtype)


def _softmax_pipeline(qi, score, vt_ref, bufs, m_ref, l_ref, acc_ref):
    set0, set1 = bufs
    n_streams = m_ref.shape[0]
    m_ref[...] = jnp.full_like(m_ref, NEG)
    l_ref[...] = jnp.zeros_like(l_ref)
    acc_ref[...] = jnp.zeros_like(acc_ref)
    set1[2][...] = jnp.zeros_like(set1[2])
    set1[3][...] = jnp.ones_like(set1[3])

    def put_scores(j, masked, z_ref, t_ref):
        for a, z in enumerate(score(j, masked)):
            z_ref[a] = z
            t_ref[a] = jnp.max(z, axis=0, keepdims=True)

    def value_update(j, p_ref, a_ref):
        vb = vt_ref[j]
        for a in range(n_streams):
            acc_ref[a] = a_ref[a] * acc_ref[a] + _dot(vb, p_ref[a])

    def softmax_pass(z_ref, t_ref, p_ref, a_ref):
        for a in range(n_streams):
            m_old = m_ref[a]
            m_new = jnp.maximum(m_old, t_ref[a])
            alpha = jnp.exp2(m_old - m_new)
            p = jnp.exp2(z_ref[a] - m_new)
            l_ref[a] = alpha * l_ref[a] + jnp.sum(p, axis=0, keepdims=True)
            m_ref[a] = m_new
            a_ref[a] = alpha
            p_ref[a] = p.astype(BF16)

    def step(n, cur, oth, scores_first=False):
        j = qi - n
        if scores_first:
            put_scores(jnp.maximum(j - 1, 0), False, oth[0], oth[1])
        value_update(jnp.minimum(j + 1, qi), oth[2], oth[3])
        if not scores_first:
            put_scores(jnp.maximum(j - 1, 0), False, oth[0], oth[1])
        softmax_pass(*cur)

    put_scores(qi, True, set0[0], set0[1])
    _run_steps(step, qi + 1, set0, set1, lambda last: value_update(0, last[2], last[3]))


def _diff_kernel(qt_ref, k_ref, vt_ref, lam_ref, g_ref, o_ref, m_ref, l_ref, acc_ref, *bufs, lam_init):
    qi = pl.program_id(1)
    tq = qt_ref.shape[1]
    tk = vt_ref.shape[2]
    qt = qt_ref[...]

    def scores(j, masked):
        kb = k_ref[pl.ds(pl.multiple_of(j * tk, tk), tk), :]
        zs = []
        for a in range(2):
            sl = slice(a * HEAD_DIM, (a + 1) * HEAD_DIM)
            z = _dot(kb[:, sl], qt[sl])
            if masked:
                key, qry = _tile_iotas(tk, tq)
                z = jnp.where(key <= qry, z, NEG)
            zs.append(z)
        return zs

    _softmax_pipeline(qi, scores, vt_ref, (bufs[:4], bufs[4:]), m_ref, l_ref, acc_ref)
    lp = lam_ref[...]
    lam = (jnp.exp(jnp.sum(lp[0:1] * lp[1:2], axis=-1, keepdims=True))
           - jnp.exp(jnp.sum(lp[2:3] * lp[3:4], axis=-1, keepdims=True)) + lam_init)
    o = acc_ref[0] * (1.0 / l_ref[0]) - lam * (acc_ref[1] * (1.0 / l_ref[1]))
    o_ref[...] = (_rms(o.T, g_ref[...]) * (1.0 - lam_init)).astype(o_ref.dtype)


def _fox_kernel(kmax_ref, qt_ref, k_ref, vt_ref, cs_ref, ct_ref, o_ref, m_ref, l_ref, acc_ref):
    h = pl.program_id(0)
    qi = pl.program_id(1)
    tq = qt_ref.shape[1]
    tk = vt_ref.shape[2]
    qt = qt_ref[...]
    ct_all = ct_ref[...]
    head = lax.broadcasted_iota(jnp.int32, ct_all.shape, 0)
    ct = jnp.sum(jnp.where(head == h, ct_all, 0.0), axis=0, keepdims=True)
    qf = qt.astype(F32)
    q_norm = jnp.sqrt(jnp.sum(qf * qf, axis=0, keepdims=True))
    m_ref[...] = jnp.full_like(m_ref, NEG)
    l_ref[...] = jnp.zeros_like(l_ref)
    acc_ref[...] = jnp.zeros_like(acc_ref)

    def tile(j, masked):
        rows = pl.ds(pl.multiple_of(j * tk, tk), tk)
        cs = pltpu.repeat(cs_ref[rows, :], tq // LANES, 1)
        z = _dot(k_ref[rows, :], qt) + ct - cs
        if masked:
            key, qry = _tile_iotas(tk, tq)
            z = jnp.where(key <= qry, z, NEG)
        m_old = m_ref[0]
        m_new = jnp.maximum(m_old, jnp.max(z, axis=0, keepdims=True))
        alpha = jnp.exp2(m_old - m_new)
        p = jnp.exp2(z - m_new)
        l_ref[0] = alpha * l_ref[0] + jnp.sum(p, axis=0, keepdims=True)
        acc_ref[0] = alpha * acc_ref[0] + _dot(vt_ref[j], p.astype(BF16))
        m_ref[0] = m_new
        jn = jnp.maximum(j - 1, 0)
        c_end = cs_ref[pl.ds(pl.multiple_of((jn + 1) * tk - 8, 8), 8), :][7:8, :]
        bound = q_norm * kmax_ref[jn, h] + ct - pltpu.repeat(c_end, tq // LANES, 1) + BOUND_SLACK
        return (jnp.max(bound - m_new) > -DEAD_LOG2).astype(jnp.int32)

    def body(carry):
        n, _ = carry
        return n + 1, tile(qi - n, False)

    lax.while_loop(lambda c: (c[0] <= qi) & (c[1] > 0), body, (jnp.int32(1), tile(qi, True)))
    o_ref[...] = (acc_ref[0] * (1.0 / l_ref[0])).T.astype(o_ref.dtype)


def _attention(kind, qt, k, vt, extra, layer):
    d, s = qt.shape
    nb, _, tk = vt.shape
    tq = ROW_TILE
    hw = 2 * HEAD_DIM if kind == 1 else HEAD_DIM
    n_heads = d // hw
    specs = [pl.BlockSpec((hw, tq), lambda h, i: (h, i)),
             pl.BlockSpec((s, hw), lambda h, i: (0, h)),
             pl.BlockSpec((nb, hw, tk), lambda h, i: (0, h, 0))]
    ns = hw // HEAD_DIM
    stats = [pltpu.VMEM((ns, 1, tq), F32)] * 2 + [pltpu.VMEM((ns, hw, tq), F32)]
    args = [qt, k, vt]
    if kind == 0:
        kern, vmem = _sb_kernel, 40
        scratch = [pltpu.VMEM((HEAD_DIM, tq), F32), pltpu.VMEM((1, tq), F32)]
    elif kind == 1:
        lam_init = 0.8 - 0.6 * math.exp(-0.3 * layer)
        kern, vmem = functools.partial(_diff_kernel, lam_init=lam_init), 48
        args += list(extra)
        specs += [pl.BlockSpec((4, HEAD_DIM), lambda h, i: (0, 0)),
                  pl.BlockSpec((1, hw), lambda h, i: (0, 0))]
        scratch = stats + [pltpu.VMEM((ns, tk, tq), F32), pltpu.VMEM((ns, 1, tq), F32),
                           pltpu.VMEM((ns, tk, tq), BF16), pltpu.VMEM((ns, 1, tq), F32)] * 2
    else:
        kern, vmem = _fox_kernel, 56
        cs, ct, kmax = extra
        args = [kmax] + args + [cs, ct]
        specs = [pl.BlockSpec(memory_space=pltpu.SMEM)] + specs
        specs += [pl.BlockSpec((None, s, LANES), lambda h, i: (h, 0, 0)),
                  pl.BlockSpec((d // HEAD_DIM, tq), lambda h, i: (0, i))]
        scratch = stats
    return pl.pallas_call(
        kern, grid=(n_heads, s // tq), in_specs=specs,
        out_specs=pl.BlockSpec((tq, hw), lambda h, i: (i, h)),
        out_shape=jax.ShapeDtypeStruct((s, d), BF16), scratch_shapes=scratch,
        compiler_params=_params(vmem, 2), name=f"attn{kind}")(*args)


def _prep_ffn(w_gate, w_up, w_down):
    return w_gate.astype(BF16), w_up.astype(BF16), w_down.astype(BF16)


def _rope_tables(s):
    inv = ROPE_THETA ** (-jnp.arange(ROT_HALF, dtype=F32) * 2.0 / ROT_DIM)
    ang = jnp.arange(s).astype(F32)[:, None] * inv[None, :]
    cos, sin = jnp.cos(ang), jnp.sin(ang)
    pad = HEAD_DIM - ROT_DIM
    zeros = jnp.zeros_like(sin)
    cos_f = jnp.concatenate([cos, cos, jnp.ones((s, pad), F32)], axis=1)
    sin_a = jnp.concatenate([-sin, zeros, jnp.zeros((s, pad), F32)], axis=1)
    sin_b = jnp.concatenate([zeros, sin, jnp.zeros((s, pad), F32)], axis=1)
    return cos.T, sin.T, cos_f, sin_a, sin_b


def kernel(x, norm_g, final_g, ffn_w_gate, ffn_w_up, ffn_w_down, w_qkv, w_o,
           diff_lambda, diff_subln_g, fox_w_f, fox_b_f):
    b, s, d = x.shape
    assert b == 1 and s % ROW_TILE == 0 and d % (2 * HEAD_DIM) == 0
    depth = w_qkv.shape[0]
    n_heads = d // HEAD_DIM
    xs = x.reshape(s, d)
    for i in range(depth):
        kind, j = i % N_MIXERS, i // N_MIXERS
        xs = _ffn(xs, norm_g[i, 0][None], *_prep_ffn(ffn_w_gate[i, 0], ffn_w_up[i, 0], ffn_w_down[i, 0]))
        wq, wk, wv = w_qkv[i][:, :d], w_qkv[i][:, d:2 * d], w_qkv[i][:, 2 * d:]
        rope = fox = None
        if kind == 1:
            rope = _rope_tables(s)
        if kind == 2:
            wf = jnp.zeros((d, LANES), F32).at[:, :n_heads].set(fox_w_f[j]).astype(BF16)
            bf = jnp.zeros((1, LANES), F32).at[0, :n_heads].set(fox_b_f[j])
            fox = (wf, bf)
        outs = _qkv(xs, norm_g[i, 1][None], wq.T.astype(BF16), wk.astype(BF16), wv.T.astype(BF16),
                    kind, rope=rope, fox=fox)
        extra = ()
        if kind == 1:
            extra = (diff_lambda[j], diff_subln_g[j][None])
        if kind == 2:
            extra = (outs[3], outs[4], outs[5].reshape(-1, LANES))
        mix = _attention(kind, outs[0], outs[1], outs[2], extra, i)
        xs = _ffn(xs, norm_g[i, 2][None], *_prep_ffn(ffn_w_gate[i, 1], ffn_w_up[i, 1], ffn_w_down[i, 1]),
                  proj=(mix, w_o[i].astype(BF16)),
                  final_g=final_g[None] if i == depth - 1 else None)
    return xs.reshape(b, s, d)
```

```python
import functools
import math

import jax
import jax.numpy as jnp
from jax import lax
from jax.experimental import pallas as pl
from jax.experimental.pallas import tpu as pltpu

F32 = jnp.float32
BF16 = jnp.bfloat16

HEAD_DIM = 128
N_MIXERS = 3
ROPE_THETA = 500000.0
ROT_DIM = HEAD_DIM // 4
ROT_HALF = ROT_DIM // 2
EPS = 1e-6
NEG = -1e30
SB_NEG = -1e4
LOG2E = 1.4426950408889634
Q_SCALE = HEAD_DIM ** -0.5 * LOG2E

LANES = 128
MXU_DIM = 256
VMEM_BYTES = 64 * 1024 * 1024

ROW_TILE = 512
DENSE_TILE = 1024
FFN_CHUNK = 256
CUM_BLOCK = MXU_DIM
DEAD_LOG2 = 160.0
BOUND_SLACK = 1.0


def _dot(a, b):
    return jnp.dot(a, b, preferred_element_type=F32)


def _dot_nt(a, b):
    return lax.dot_general(a, b, (((1,), (1,)), ((), ())), preferred_element_type=F32)


def _rms(x, g):
    return x * lax.rsqrt(jnp.mean(x * x, axis=-1, keepdims=True) + EPS) * g


def _params(vmem_mib, n_axes):
    return pltpu.CompilerParams(
        dimension_semantics=("arbitrary",) * n_axes,
        vmem_limit_bytes=vmem_mib * 1024 * 1024)


def _resident(shape, index_map):
    return pl.BlockSpec(shape, index_map, pipeline_mode=pl.Buffered(1))


def _ffn_kernel(*refs, has_proj, has_final, n_chunks):
    it = iter(refs)
    x_ref = next(it)
    if has_proj:
        mix_ref, wo_ref = next(it), next(it)
    g_ref, wg_ref, wu_ref, wd_ref = next(it), next(it), next(it), next(it)
    if has_final:
        fg_ref = next(it)
    o_ref, a_ref = next(it), next(it)

    x = x_ref[...]
    if has_proj:
        x = x + _dot(mix_ref[...], wo_ref[...])
    h = _rms(x, g_ref[...]).astype(BF16)
    for c in range(n_chunks):
        cols = slice(c * FFN_CHUNK, (c + 1) * FFN_CHUNK)
        gate = _dot(h, wg_ref[:, cols])
        up = _dot(h, wu_ref[:, cols])
        a_ref[:, cols] = (gate * jax.nn.sigmoid(gate) * up).astype(BF16)
    y = x + 0.5 * _dot(a_ref[...], wd_ref[...])
    if has_final:
        y = _rms(y, fg_ref[...])
    o_ref[...] = y


def _ffn(x, g, wg, wu, wd, proj=None, final_g=None):
    s, d = x.shape
    f = wd.shape[0]
    tm = ROW_TILE
    row = lambda i: (i, 0)
    const = lambda i: (0, 0)
    args, specs = [x], [pl.BlockSpec((tm, d), row)]
    if proj is not None:
        mix, wo = proj
        args += [mix, wo]
        specs += [pl.BlockSpec((tm, d), row), _resident((d, d), const)]
    args += [g, wg, wu, wd]
    specs += [_resident((1, d), const), _resident((d, f), const), _resident((d, f), const),
              _resident((f, d), const)]
    if final_g is not None:
        args.append(final_g)
        specs.append(_resident((1, d), const))
    kern = functools.partial(_ffn_kernel, has_proj=proj is not None, has_final=final_g is not None,
                             n_chunks=f // FFN_CHUNK)
    return pl.pallas_call(
        kern, grid=(s // tm,), in_specs=specs, out_specs=pl.BlockSpec((tm, d), row),
        out_shape=jax.ShapeDtypeStruct((s, d), F32),
        scratch_shapes=[pltpu.VMEM((tm, f), BF16)],
        compiler_params=_params(48, 1), name="ffn")(*args)


def _qkv_kernel(*refs, kind, n_heads):
    it = iter(refs)
    x_ref, g_ref, wqt_ref, wk_ref, wvt_ref = (next(it) for _ in range(5))
    if kind == 1:
        cost_ref, sint_ref, cosf_ref, sina_ref, sinb_ref = (next(it) for _ in range(5))
    if kind == 2:
        wf_ref, bf_ref = next(it), next(it)
    qt_ref, k_ref, vt_ref = next(it), next(it), next(it)
    if kind == 2:
        cs_ref, ct_ref, kmax_ref, carry_ref, kcarry_ref = (next(it) for _ in range(5))

    h = _rms(x_ref[...], g_ref[...]).astype(BF16)
    qt = _dot_nt(wqt_ref[...], h) * Q_SCALE
    k = _dot(h, wk_ref[...])
    vt_ref[...] = _dot_nt(wvt_ref[...], h).astype(BF16)

    if kind != 1:
        qt_ref[...] = qt.astype(BF16)
        k_ref[...] = k.astype(BF16)
    else:
        cos_t, sin_t = cost_ref[...], sint_ref[...]
        cos_f, sin_a, sin_b = cosf_ref[...], sina_ref[...], sinb_ref[...]
        for hh in range(n_heads):
            r0 = hh * HEAD_DIM
            x1, x2 = qt[r0:r0 + ROT_HALF], qt[r0 + ROT_HALF:r0 + ROT_DIM]
            qt_ref[r0:r0 + ROT_HALF, :] = (x1 * cos_t - x2 * sin_t).astype(BF16)
            qt_ref[r0 + ROT_HALF:r0 + ROT_DIM, :] = (x2 * cos_t + x1 * sin_t).astype(BF16)
            qt_ref[r0 + ROT_DIM:r0 + HEAD_DIM, :] = qt[r0 + ROT_DIM:r0 + HEAD_DIM].astype(BF16)
            kh = k[:, r0:r0 + HEAD_DIM]
            rot = (kh * cos_f + pltpu.roll(kh, HEAD_DIM - ROT_HALF, 1) * sin_a
                   + pltpu.roll(kh, ROT_HALF, 1) * sin_b)
            k_ref[:, r0:r0 + HEAD_DIM] = rot.astype(BF16)

    if kind == 2:
        tm = h.shape[0]

        @pl.when(pl.program_id(0) == 0)
        def _():
            carry_ref[...] = jnp.zeros_like(carry_ref)
            kcarry_ref[...] = jnp.zeros_like(kcarry_ref)

        kb = k.astype(BF16).astype(F32)
        lane = lax.broadcasted_iota(jnp.int32, (1, LANES), 1)
        kmax = kcarry_ref[...]
        for hh in range(n_heads):
            kh = kb[:, hh * HEAD_DIM:(hh + 1) * HEAD_DIM]
            n2 = jnp.max(jnp.sum(kh * kh, axis=1, keepdims=True), axis=0, keepdims=True)
            kmax = jnp.where(lane == hh, jnp.maximum(kmax, jnp.sqrt(n2)), kmax)
        kcarry_ref[...] = kmax
        kmax_ref[...] = kmax

        fl = _dot(h, wf_ref[...]) + bf_ref[...]
        lf = jnp.minimum(fl, 0.0) - jnp.log(1.0 + jnp.exp(-jnp.abs(fl)))
        r = lax.broadcasted_iota(jnp.int32, (tm, tm), 0)
        c = lax.broadcasted_iota(jnp.int32, (tm, tm), 1)
        tri = jnp.where(c <= r, 1.0, 0.0).astype(BF16)
        p0 = lf.astype(BF16)
        r1 = lf - p0.astype(F32)
        p1 = r1.astype(BF16)
        p2 = (r1 - p1.astype(F32)).astype(BF16)
        csum = _dot(tri, p0) + _dot(tri, p1) + _dot(tri, p2) + carry_ref[...]
        carry_ref[...] = csum[tm - 1:tm, :]
        c2 = csum * LOG2E
        ct_ref[...] = c2.T[:n_heads, :]
        for hh in range(n_heads):
            cs_ref[hh] = jnp.broadcast_to(c2[:, hh:hh + 1], (tm, LANES))


def _qkv(x, g, wqt, wk, wvt, kind, tm, rope=None, fox=None):
    s, d = x.shape
    nb = s // tm
    n_heads = d // HEAD_DIM
    row = lambda i: (i, 0)
    col = lambda i: (0, i)
    const = lambda i: (0, 0)
    args = [x, g, wqt, wk, wvt]
    specs = [pl.BlockSpec((tm, d), row), _resident((1, d), const),
             _resident((d, d), const), _resident((d, d), const), _resident((d, d), const)]
    out_shape = [jax.ShapeDtypeStruct((d, s), BF16), jax.ShapeDtypeStruct((s, d), BF16),
                 jax.ShapeDtypeStruct((nb, d, tm), BF16)]
    out_specs = [pl.BlockSpec((d, tm), col), pl.BlockSpec((tm, d), row),
                 pl.BlockSpec((None, d, tm), lambda i: (i, 0, 0))]
    scratch = []
    if kind == 1:
        args += list(rope)
        specs += [pl.BlockSpec((ROT_HALF, tm), col)] * 2 + [pl.BlockSpec((tm, HEAD_DIM), row)] * 3
    if kind == 2:
        args += list(fox)
        specs += [_resident((d, LANES), const), _resident((1, LANES), const)]
        out_shape += [jax.ShapeDtypeStruct((n_heads, s, LANES), F32),
                      jax.ShapeDtypeStruct((n_heads, s), F32),
                      jax.ShapeDtypeStruct((nb, 1, LANES), F32)]
        out_specs += [pl.BlockSpec((n_heads, tm, LANES), lambda i: (0, i, 0)),
                      pl.BlockSpec((n_heads, tm), col),
                      pl.BlockSpec((None, 1, LANES), lambda i: (i, 0, 0))]
        scratch = [pltpu.VMEM((1, LANES), F32)] * 2
    kern = functools.partial(_qkv_kernel, kind=kind, n_heads=n_heads)
    return pl.pallas_call(
        kern, grid=(nb,), in_specs=specs, out_specs=out_specs, out_shape=out_shape,
        scratch_shapes=scratch, compiler_params=_params(40 if tm <= ROW_TILE else 56, 1),
        name=f"qkv{kind}")(*args)


def _tile_iotas(tk, tq):
    return (lax.broadcasted_iota(jnp.int32, (tk, tq), 0),
            lax.broadcasted_iota(jnp.int32, (tk, tq), 1))


def _run_steps(step, n_tiles, set0, set1):
    def pair(i, carry):
        step(2 * i, set0, set1)
        step(2 * i + 1, set1, set0)
        return carry

    lax.fori_loop(0, lax.shift_right_logical(n_tiles - 1, 1), pair, 0)
    odd = lax.bitwise_and(n_tiles, 1) == 1

    @pl.when(odd)
    def _():
        step(n_tiles - 1, set0, set1, last=True)

    @pl.when(jnp.logical_not(odd))
    def _():
        step(n_tiles - 2, set0, set1)
        step(n_tiles - 1, set1, set0, last=True)


def _sb_kernel(qt_ref, k_ref, vt_ref, o_ref, acc_ref, run_ref):
    qi = pl.program_id(1)
    tq = qt_ref.shape[1]
    tk = vt_ref.shape[2]
    qt = qt_ref[...]
    d_i, c_i = _tile_iotas(CUM_BLOCK, CUM_BLOCK)
    later = jnp.where(c_i > d_i, 1.0, 0.0).astype(BF16)
    acc_ref[...] = jnp.zeros_like(acc_ref)
    run_ref[...] = jnp.zeros_like(run_ref)

    def tile(j, masked):
        kb = k_ref[pl.ds(pl.multiple_of(j * tk, tk), tk), :]
        z = _dot(kb, qt)
        if masked:
            key, qry = _tile_iotas(tk, tq)
            z = jnp.where(key < qry, z, SB_NEG * LOG2E)
        sp = jnp.maximum(z, 0.0) + jnp.log(1.0 + jnp.exp2(-jnp.abs(z))) * LOG2E
        vb = vt_ref[j]
        run = run_ref[...]
        acc = acc_ref[...]
        for b in reversed(range(tk // CUM_BLOCK)):
            sl = slice(b * CUM_BLOCK, (b + 1) * CUM_BLOCK)
            sp_b = sp[sl]
            after = _dot(later, sp_b.astype(BF16)) + run
            w = jnp.exp2(z[sl] - sp_b - after).astype(BF16)
            acc = acc + _dot(vb[:, sl], w)
            run = run + jnp.sum(sp_b, axis=0, keepdims=True)
        run_ref[...] = run
        acc_ref[...] = acc
        return (jnp.min(run) < DEAD_LOG2).astype(jnp.int32)

    def body(carry):
        n, _ = carry
        return n + 1, tile(qi - n, False)

    lax.while_loop(lambda c: (c[0] <= qi) & (c[1] > 0), body, (jnp.int32(1), tile(qi, True)))
    o_ref[...] = acc_ref[...].T.astype(o_ref.dtype)


def _softmax_pipeline(qi, score, vt_ref, bufs, m_ref, l_ref, acc_ref):
    set0, set1 = bufs
    n_streams = m_ref.shape[0]
    m_ref[...] = jnp.full_like(m_ref, NEG)
    l_ref[...] = jnp.zeros_like(l_ref)
    acc_ref[...] = jnp.zeros_like(acc_ref)

    def put_scores(j, masked, z_ref, t_ref):
        for a, z in enumerate(score(j, masked)):
            z_ref[a] = z
            t_ref[a] = jnp.max(z, axis=0, keepdims=True)

    def step(n, cur, oth, last=False):
        j = qi - n
        if not last:
            put_scores(j - 1, False, *oth)
        z_ref, t_ref = cur
        vb = vt_ref[j]
        for a in range(n_streams):
            m_old = m_ref[a]
            m_new = jnp.maximum(m_old, t_ref[a])
            alpha = jnp.exp2(m_old - m_new)
            p = jnp.exp2(z_ref[a] - m_new)
            l_ref[a] = alpha * l_ref[a] + jnp.sum(p, axis=0, keepdims=True)
            m_ref[a] = m_new
            acc_ref[a] = alpha * acc_ref[a] + _dot(vb, p.astype(BF16))

    put_scores(qi, True, *set0)
    _run_steps(step, qi + 1, set0, set1)


def _diff_kernel(qt_ref, k_ref, vt_ref, lam_ref, g_ref, o_ref, m_ref, l_ref, acc_ref, *bufs, lam_init):
    qi = pl.program_id(1)
    tq = qt_ref.shape[1]
    tk = vt_ref.shape[2]
    qt = qt_ref[...]

    def scores(j, masked):
        kb = k_ref[pl.ds(pl.multiple_of(j * tk, tk), tk), :]
        zs = []
        for a in range(2):
            sl = slice(a * HEAD_DIM, (a + 1) * HEAD_DIM)
            z = _dot(kb[:, sl], qt[sl])
            if masked:
                key, qry = _tile_iotas(tk, tq)
                z = jnp.where(key <= qry, z, NEG)
            zs.append(z)
        return zs

    _softmax_pipeline(qi, scores, vt_ref, (bufs[:2], bufs[2:]), m_ref, l_ref, acc_ref)
    lp = lam_ref[...]
    lam = (jnp.exp(jnp.sum(lp[0:1] * lp[1:2], axis=-1, keepdims=True))
           - jnp.exp(jnp.sum(lp[2:3] * lp[3:4], axis=-1, keepdims=True)) + lam_init)
    o = acc_ref[0] * (1.0 / l_ref[0]) - lam * (acc_ref[1] * (1.0 / l_ref[1]))
    o_ref[...] = (_rms(o.T, g_ref[...]) * (1.0 - lam_init)).astype(o_ref.dtype)


def _fox_kernel(kmax_ref, qt_ref, k_ref, vt_ref, cs_ref, ct_ref, o_ref, m_ref, l_ref, acc_ref):
    h = pl.program_id(0)
    qi = pl.program_id(1)
    tq = qt_ref.shape[1]
    tk = vt_ref.shape[2]
    qt = qt_ref[...]
    ct_all = ct_ref[...]
    head = lax.broadcasted_iota(jnp.int32, ct_all.shape, 0)
    ct = jnp.sum(jnp.where(head == h, ct_all, 0.0), axis=0, keepdims=True)
    qf = qt.astype(F32)
    q_norm = jnp.sqrt(jnp.sum(qf * qf, axis=0, keepdims=True))
    m_ref[...] = jnp.full_like(m_ref, NEG)
    l_ref[...] = jnp.zeros_like(l_ref)
    acc_ref[...] = jnp.zeros_like(acc_ref)

    def tile(j, masked):
        rows = pl.ds(pl.multiple_of(j * tk, tk), tk)
        cs = pltpu.repeat(cs_ref[rows, :], tq // LANES, 1)
        z = _dot(k_ref[rows, :], qt) + ct - cs
        if masked:
            key, qry = _tile_iotas(tk, tq)
            z = jnp.where(key <= qry, z, NEG)
        m_old = m_ref[0]
        m_new = jnp.maximum(m_old, jnp.max(z, axis=0, keepdims=True))
        alpha = jnp.exp2(m_old - m_new)
        p = jnp.exp2(z - m_new)
        l_ref[0] = alpha * l_ref[0] + jnp.sum(p, axis=0, keepdims=True)
        acc_ref[0] = alpha * acc_ref[0] + _dot(vt_ref[j], p.astype(BF16))
        m_ref[0] = m_new
        jn = jnp.maximum(j - 1, 0)
        c_end = cs_ref[pl.ds(pl.multiple_of((jn + 1) * tk - 8, 8), 8), :][7:8, :]
        bound = q_norm * kmax_ref[jn, h] + ct - pltpu.repeat(c_end, tq // LANES, 1) + BOUND_SLACK
        return (jnp.max(bound - m_new) > -DEAD_LOG2).astype(jnp.int32)

    def body(carry):
        n, _ = carry
        return n + 1, tile(qi - n, False)

    lax.while_loop(lambda c: (c[0] <= qi) & (c[1] > 0), body, (jnp.int32(1), tile(qi, True)))
    o_ref[...] = (acc_ref[0] * (1.0 / l_ref[0])).T.astype(o_ref.dtype)


def _attention(kind, qt, k, vt, extra, layer):
    d, s = qt.shape
    nb, _, tk = vt.shape
    tq = tk
    hw = 2 * HEAD_DIM if kind == 1 else HEAD_DIM
    n_heads = d // hw
    per_head = _resident if kind == 1 else pl.BlockSpec
    specs = [pl.BlockSpec((hw, tq), lambda h, i: (h, i)),
             per_head((s, hw), lambda h, i: (0, h)),
             per_head((nb, hw, tk), lambda h, i: (0, h, 0))]
    ns = hw // HEAD_DIM
    stats = [pltpu.VMEM((ns, 1, tq), F32)] * 2 + [pltpu.VMEM((ns, hw, tq), F32)]
    args = [qt, k, vt]
    if kind == 0:
        kern, vmem = _sb_kernel, 40
        scratch = [pltpu.VMEM((HEAD_DIM, tq), F32), pltpu.VMEM((1, tq), F32)]
    elif kind == 1:
        lam_init = 0.8 - 0.6 * math.exp(-0.3 * layer)
        kern, vmem = functools.partial(_diff_kernel, lam_init=lam_init), 58
        args += list(extra)
        specs += [pl.BlockSpec((4, HEAD_DIM), lambda h, i: (0, 0)),
                  pl.BlockSpec((1, hw), lambda h, i: (0, 0))]
        scratch = stats + [pltpu.VMEM((ns, tk, tq), F32), pltpu.VMEM((ns, 1, tq), F32)] * 2
    else:
        kern, vmem = _fox_kernel, 56
        cs, ct, kmax = extra
        args = [kmax] + args + [cs, ct]
        specs = [pl.BlockSpec(memory_space=pltpu.SMEM)] + specs
        specs += [pl.BlockSpec((None, s, LANES), lambda h, i: (h, 0, 0)),
                  pl.BlockSpec((d // HEAD_DIM, tq), lambda h, i: (0, i))]
        scratch = stats
    return pl.pallas_call(
        kern, grid=(n_heads, s // tq), in_specs=specs,
        out_specs=pl.BlockSpec((tq, hw), lambda h, i: (i, h)),
        out_shape=jax.ShapeDtypeStruct((s, d), BF16), scratch_shapes=scratch,
        compiler_params=_params(vmem, 2), name=f"attn{kind}")(*args)


def _prep_ffn(w_gate, w_up, w_down):
    return w_gate.astype(BF16), w_up.astype(BF16), w_down.astype(BF16)


def _rope_tables(s):
    inv = ROPE_THETA ** (-jnp.arange(ROT_HALF, dtype=F32) * 2.0 / ROT_DIM)
    ang = jnp.arange(s).astype(F32)[:, None] * inv[None, :]
    cos, sin = jnp.cos(ang), jnp.sin(ang)
    pad = HEAD_DIM - ROT_DIM
    zeros = jnp.zeros_like(sin)
    cos_f = jnp.concatenate([cos, cos, jnp.ones((s, pad), F32)], axis=1)
    sin_a = jnp.concatenate([-sin, zeros, jnp.zeros((s, pad), F32)], axis=1)
    sin_b = jnp.concatenate([zeros, sin, jnp.zeros((s, pad), F32)], axis=1)
    return cos.T, sin.T, cos_f, sin_a, sin_b


def kernel(x, norm_g, final_g, ffn_w_gate, ffn_w_up, ffn_w_down, w_qkv, w_o,
           diff_lambda, diff_subln_g, fox_w_f, fox_b_f):
    b, s, d = x.shape
    assert b == 1 and s % DENSE_TILE == 0 and d % (2 * HEAD_DIM) == 0
    depth = w_qkv.shape[0]
    n_heads = d // HEAD_DIM
    xs = x.reshape(s, d)
    for i in range(depth):
        kind, j = i % N_MIXERS, i // N_MIXERS
        xs = _ffn(xs, norm_g[i, 0][None], *_prep_ffn(ffn_w_gate[i, 0], ffn_w_up[i, 0], ffn_w_down[i, 0]))
        wq, wk, wv = w_qkv[i][:, :d], w_qkv[i][:, d:2 * d], w_qkv[i][:, 2 * d:]
        rope = fox = None
        if kind == 1:
            rope = _rope_tables(s)
        if kind == 2:
            wf = jnp.zeros((d, LANES), F32).at[:, :n_heads].set(fox_w_f[j]).astype(BF16)
            bf = jnp.zeros((1, LANES), F32).at[0, :n_heads].set(fox_b_f[j])
            fox = (wf, bf)
        outs = _qkv(xs, norm_g[i, 1][None], wq.T.astype(BF16), wk.astype(BF16), wv.T.astype(BF16),
                    kind, DENSE_TILE if kind == 1 else ROW_TILE, rope=rope, fox=fox)
        extra = ()
        if kind == 1:
            extra = (diff_lambda[j], diff_subln_g[j][None])
        if kind == 2:
            extra = (outs[3], outs[4], outs[5].reshape(-1, LANES))
        mix = _attention(kind, outs[0], outs[1], outs[2], extra, i)
        xs = _ffn(xs, norm_g[i, 2][None], *_prep_ffn(ffn_w_gate[i, 1], ffn_w_up[i, 1], ffn_w_down[i, 1]),
                  proj=(mix, w_o[i].astype(BF16)),
                  final_g=final_g[None] if i == depth - 1 else None)
    return xs.reshape(b, s, d)
```

```python
import functools
import math

import jax
import jax.numpy as jnp
from jax import lax
from jax.experimental import pallas as pl
from jax.experimental.pallas import tpu as pltpu

F32 = jnp.float32
BF16 = jnp.bfloat16

HEAD_DIM = 128
N_MIXERS = 3
ROPE_THETA = 500000.0
ROT_DIM = HEAD_DIM // 4
ROT_HALF = ROT_DIM // 2
EPS = 1e-6
NEG = -1e30
SB_NEG = -1e4
LOG2E = 1.4426950408889634
Q_SCALE = HEAD_DIM ** -0.5 * LOG2E

LANES = 128
MXU_DIM = 256
VMEM_BYTES = 64 * 1024 * 1024

ROW_TILE = 512
DENSE_TILE = 1024
FFN_CHUNK = 256
CUM_BLOCK = MXU_DIM
DEAD_LOG2 = 160.0
BOUND_SLACK = 1.0


def _dot(a, b):
    return jnp.dot(a, b, preferred_element_type=F32)


def _dot_nt(a, b):
    return lax.dot_general(a, b, (((1,), (1,)), ((), ())), preferred_element_type=F32)


def _rms(x, g):
    return x * lax.rsqrt(jnp.mean(x * x, axis=-1, keepdims=True) + EPS) * g


def _params(vmem_mib, n_axes):
    return pltpu.CompilerParams(
        dimension_semantics=("arbitrary",) * n_axes,
        vmem_limit_bytes=vmem_mib * 1024 * 1024)


def _resident(shape, index_map):
    return pl.BlockSpec(shape, index_map, pipeline_mode=pl.Buffered(1))


def _ffn_kernel(*refs, has_proj, has_final, n_chunks):
    it = iter(refs)
    x_ref = next(it)
    if has_proj:
        mix_ref, wo_ref = next(it), next(it)
    g_ref, wg_ref, wu_ref, wd_ref = next(it), next(it), next(it), next(it)
    if has_final:
        fg_ref = next(it)
    o_ref, a_ref = next(it), next(it)

    x = x_ref[...]
    if has_proj:
        x = x + _dot(mix_ref[...], wo_ref[...])
    h = _rms(x, g_ref[...]).astype(BF16)
    for c in range(n_chunks):
        cols = slice(c * FFN_CHUNK, (c + 1) * FFN_CHUNK)
        gate = _dot(h, wg_ref[:, cols])
        up = _dot(h, wu_ref[:, cols])
        a_ref[:, cols] = (gate * jax.nn.sigmoid(gate) * up).astype(BF16)
    y = x + 0.5 * _dot(a_ref[...], wd_ref[...])
    if has_final:
        y = _rms(y, fg_ref[...])
    o_ref[...] = y


def _ffn(x, g, wg, wu, wd, proj=None, final_g=None):
    s, d = x.shape
    f = wd.shape[0]
    tm = ROW_TILE
    row = lambda i: (i, 0)
    const = lambda i: (0, 0)
    args, specs = [x], [pl.BlockSpec((tm, d), row)]
    if proj is not None:
        mix, wo = proj
        args += [mix, wo]
        specs += [pl.BlockSpec((tm, d), row), _resident((d, d), const)]
    args += [g, wg, wu, wd]
    specs += [_resident((1, d), const), _resident((d, f), const), _resident((d, f), const),
              _resident((f, d), const)]
    if final_g is not None:
        args.append(final_g)
        specs.append(_resident((1, d), const))
    kern = functools.partial(_ffn_kernel, has_proj=proj is not None, has_final=final_g is not None,
                             n_chunks=f // FFN_CHUNK)
    return pl.pallas_call(
        kern, grid=(s // tm,), in_specs=specs, out_specs=pl.BlockSpec((tm, d), row),
        out_shape=jax.ShapeDtypeStruct((s, d), F32),
        scratch_shapes=[pltpu.VMEM((tm, f), BF16)],
        compiler_params=_params(48, 1), name="ffn")(*args)


def _qkv_kernel(*refs, kind, n_heads):
    it = iter(refs)
    x_ref, g_ref, wqt_ref, wk_ref, wvt_ref = (next(it) for _ in range(5))
    if kind == 1:
        cost_ref, sint_ref, cosf_ref, sina_ref, sinb_ref = (next(it) for _ in range(5))
    if kind == 2:
        wf_ref, bf_ref = next(it), next(it)
    qt_ref, k_ref, vt_ref = next(it), next(it), next(it)
    if kind == 2:
        cs_ref, ct_ref, kmax_ref, carry_ref, kcarry_ref = (next(it) for _ in range(5))

    h = _rms(x_ref[...], g_ref[...]).astype(BF16)
    qt = _dot_nt(wqt_ref[...], h) * Q_SCALE
    k = _dot(h, wk_ref[...])
    vt_ref[...] = _dot_nt(wvt_ref[...], h).astype(BF16)

    if kind != 1:
        qt_ref[...] = qt.astype(BF16)
        k_ref[...] = k.astype(BF16)
    else:
        cos_t, sin_t = cost_ref[...], sint_ref[...]
        cos_f, sin_a, sin_b = cosf_ref[...], sina_ref[...], sinb_ref[...]
        for hh in range(n_heads):
            r0 = hh * HEAD_DIM
            x1, x2 = qt[r0:r0 + ROT_HALF], qt[r0 + ROT_HALF:r0 + ROT_DIM]
            qt_ref[r0:r0 + ROT_HALF, :] = (x1 * cos_t - x2 * sin_t).astype(BF16)
            qt_ref[r0 + ROT_HALF:r0 + ROT_DIM, :] = (x2 * cos_t + x1 * sin_t).astype(BF16)
            qt_ref[r0 + ROT_DIM:r0 + HEAD_DIM, :] = qt[r0 + ROT_DIM:r0 + HEAD_DIM].astype(BF16)
            kh = k[:, r0:r0 + HEAD_DIM]
            rot = (kh * cos_f + pltpu.roll(kh, HEAD_DIM - ROT_HALF, 1) * sin_a
                   + pltpu.roll(kh, ROT_HALF, 1) * sin_b)
            k_ref[:, r0:r0 + HEAD_DIM] = rot.astype(BF16)

    if kind == 2:
        tm = h.shape[0]

        @pl.when(pl.program_id(0) == 0)
        def _():
            carry_ref[...] = jnp.zeros_like(carry_ref)
            kcarry_ref[...] = jnp.zeros_like(kcarry_ref)

        kb = k.astype(BF16).astype(F32)
        lane = lax.broadcasted_iota(jnp.int32, (1, LANES), 1)
        kmax = kcarry_ref[...]
        for hh in range(n_heads):
            kh = kb[:, hh * HEAD_DIM:(hh + 1) * HEAD_DIM]
            n2 = jnp.max(jnp.sum(kh * kh, axis=1, keepdims=True), axis=0, keepdims=True)
            kmax = jnp.where(lane == hh, jnp.maximum(kmax, jnp.sqrt(n2)), kmax)
        kcarry_ref[...] = kmax
        kmax_ref[...] = kmax

        fl = _dot(h, wf_ref[...]) + bf_ref[...]
        lf = jnp.minimum(fl, 0.0) - jnp.log(1.0 + jnp.exp(-jnp.abs(fl)))
        r = lax.broadcasted_iota(jnp.int32, (tm, tm), 0)
        c = lax.broadcasted_iota(jnp.int32, (tm, tm), 1)
        tri = jnp.where(c <= r, 1.0, 0.0).astype(BF16)
        p0 = lf.astype(BF16)
        r1 = lf - p0.astype(F32)
        p1 = r1.astype(BF16)
        p2 = (r1 - p1.astype(F32)).astype(BF16)
        csum = _dot(tri, p0) + _dot(tri, p1) + _dot(tri, p2) + carry_ref[...]
        carry_ref[...] = csum[tm - 1:tm, :]
        c2 = csum * LOG2E
        ct_ref[...] = c2.T[:n_heads, :]
        for hh in range(n_heads):
            cs_ref[hh] = jnp.broadcast_to(c2[:, hh:hh + 1], (tm, LANES))


def _qkv(x, g, wqt, wk, wvt, kind, tm, rope=None, fox=None):
    s, d = x.shape
    nb = s // tm
    n_heads = d // HEAD_DIM
    row = lambda i: (i, 0)
    col = lambda i: (0, i)
    const = lambda i: (0, 0)
    args = [x, g, wqt, wk, wvt]
    specs = [pl.BlockSpec((tm, d), row), _resident((1, d), const),
             _resident((d, d), const), _resident((d, d), const), _resident((d, d), const)]
    out_shape = [jax.ShapeDtypeStruct((d, s), BF16), jax.ShapeDtypeStruct((s, d), BF16),
                 jax.ShapeDtypeStruct((nb, d, tm), BF16)]
    out_specs = [pl.BlockSpec((d, tm), col), pl.BlockSpec((tm, d), row),
                 pl.BlockSpec((None, d, tm), lambda i: (i, 0, 0))]
    scratch = []
    if kind == 1:
        args += list(rope)
        specs += [pl.BlockSpec((ROT_HALF, tm), col)] * 2 + [pl.BlockSpec((tm, HEAD_DIM), row)] * 3
    if kind == 2:
        args += list(fox)
        specs += [_resident((d, LANES), const), _resident((1, LANES), const)]
        out_shape += [jax.ShapeDtypeStruct((n_heads, s, LANES), F32),
                      jax.ShapeDtypeStruct((n_heads, s), F32),
                      jax.ShapeDtypeStruct((nb, 1, LANES), F32)]
        out_specs += [pl.BlockSpec((n_heads, tm, LANES), lambda i: (0, i, 0)),
                      pl.BlockSpec((n_heads, tm), col),
                      pl.BlockSpec((None, 1, LANES), lambda i: (i, 0, 0))]
        scratch = [pltpu.VMEM((1, LANES), F32)] * 2
    kern = functools.partial(_qkv_kernel, kind=kind, n_heads=n_heads)
    return pl.pallas_call(
        kern, grid=(nb,), in_specs=specs, out_specs=out_specs, out_shape=out_shape,
        scratch_shapes=scratch, compiler_params=_params(40 if tm <= ROW_TILE else 56, 1),
        name=f"qkv{kind}")(*args)


def _tile_iotas(tk, tq):
    return (lax.broadcasted_iota(jnp.int32, (tk, tq), 0),
            lax.broadcasted_iota(jnp.int32, (tk, tq), 1))


def _run_steps(step, n_tiles, set0, set1):
    def pair(i, carry):
        step(2 * i, set0, set1)
        step(2 * i + 1, set1, set0)
        return carry

    lax.fori_loop(0, lax.shift_right_logical(n_tiles - 1, 1), pair, 0)
    odd = lax.bitwise_and(n_tiles, 1) == 1

    @pl.when(odd)
    def _():
        step(n_tiles - 1, set0, set1, last=True)

    @pl.when(jnp.logical_not(odd))
    def _():
        step(n_tiles - 2, set0, set1)
        step(n_tiles - 1, set1, set0, last=True)


def _sb_kernel(qt_ref, k_ref, vt_ref, o_ref, acc_ref, run_ref):
    qi = pl.program_id(1)
    tq = qt_ref.shape[1]
    tk = vt_ref.shape[2]
    n_heads = acc_ref.shape[0]
    qt = qt_ref[...]
    d_i, c_i = _tile_iotas(CUM_BLOCK, CUM_BLOCK)
    later = jnp.where(c_i > d_i, 1.0, 0.0).astype(BF16)
    acc_ref[...] = jnp.zeros_like(acc_ref)
    run_ref[...] = jnp.zeros_like(run_ref)
    heads = [slice(a * HEAD_DIM, (a + 1) * HEAD_DIM) for a in range(n_heads)]

    def scores(j, masked):
        kb = k_ref[pl.ds(pl.multiple_of(j * tk, tk), tk), :]
        zs = [_dot(kb[:, hd], qt[hd]) for hd in heads]
        if masked:
            key, qry = _tile_iotas(tk, tq)
            zs = [jnp.where(key < qry, z, SB_NEG * LOG2E) for z in zs]
        return zs

    def consume(zs, j):
        vb = vt_ref[j]
        low = None
        for a, z in enumerate(zs):
            sp = jnp.maximum(z, 0.0) + jnp.log(1.0 + jnp.exp2(-jnp.abs(z))) * LOG2E
            run = run_ref[a]
            acc = acc_ref[a]
            for b in reversed(range(tk // CUM_BLOCK)):
                sl = slice(b * CUM_BLOCK, (b + 1) * CUM_BLOCK)
                sp_b = sp[sl]
                after = _dot(later, sp_b.astype(BF16)) + run
                w = jnp.exp2(z[sl] - sp_b - after).astype(BF16)
                acc = acc + _dot(vb[heads[a], sl], w)
                run = run + jnp.sum(sp_b, axis=0, keepdims=True)
            run_ref[a] = run
            acc_ref[a] = acc
            low = run if low is None else jnp.minimum(low, run)
        return (jnp.min(low) < DEAD_LOG2).astype(jnp.int32)

    z_diag = scores(qi, True)
    z_prev = [jnp.where(qi > 0, z, SB_NEG * LOG2E) for z in scores(jnp.maximum(qi - 1, 0), False)]
    consume(z_diag, qi)
    live = consume(z_prev, jnp.maximum(qi - 1, 0))

    def body(carry):
        n, _ = carry
        return n + 1, consume(scores(qi - n, False), qi - n)

    lax.while_loop(lambda c: (c[0] <= qi) & (c[1] > 0), body, (jnp.int32(2), live))
    for a in range(n_heads):
        o_ref[:, heads[a]] = acc_ref[a].T.astype(o_ref.dtype)


def _softmax_pipeline(qi, score, vt_ref, bufs, m_ref, l_ref, acc_ref):
    set0, set1 = bufs
    n_streams = m_ref.shape[0]
    m_ref[...] = jnp.full_like(m_ref, NEG)
    l_ref[...] = jnp.zeros_like(l_ref)
    acc_ref[...] = jnp.zeros_like(acc_ref)

    def put_scores(j, masked, z_ref, t_ref):
        for a, z in enumerate(score(j, masked)):
            z_ref[a] = z
            t_ref[a] = jnp.max(z, axis=0, keepdims=True)

    def step(n, cur, oth, last=False):
        j = qi - n
        if not last:
            put_scores(j - 1, False, *oth)
        z_ref, t_ref = cur
        vb = vt_ref[j]
        for a in range(n_streams):
            m_old = m_ref[a]
            m_new = jnp.maximum(m_old, t_ref[a])
            alpha = jnp.exp2(m_old - m_new)
            p = jnp.exp2(z_ref[a] - m_new)
            l_ref[a] = alpha * l_ref[a] + jnp.sum(p, axis=0, keepdims=True)
            m_ref[a] = m_new
            acc_ref[a] = alpha * acc_ref[a] + _dot(vb, p.astype(BF16))

    put_scores(qi, True, *set0)
    _run_steps(step, qi + 1, set0, set1)


def _diff_kernel(qt_ref, k_ref, vt_ref, lam_ref, g_ref, o_ref, m_ref, l_ref, acc_ref, *bufs, lam_init):
    qi = pl.program_id(1)
    tq = qt_ref.shape[1]
    tk = vt_ref.shape[2]
    qt = qt_ref[...]

    def scores(j, masked):
        kb = k_ref[pl.ds(pl.multiple_of(j * tk, tk), tk), :]
        zs = []
        for a in range(2):
            sl = slice(a * HEAD_DIM, (a + 1) * HEAD_DIM)
            z = _dot(kb[:, sl], qt[sl])
            if masked:
                key, qry = _tile_iotas(tk, tq)
                z = jnp.where(key <= qry, z, NEG)
            zs.append(z)
        return zs

    _softmax_pipeline(qi, scores, vt_ref, (bufs[:2], bufs[2:]), m_ref, l_ref, acc_ref)
    lp = lam_ref[...]
    lam = (jnp.exp(jnp.sum(lp[0:1] * lp[1:2], axis=-1, keepdims=True))
           - jnp.exp(jnp.sum(lp[2:3] * lp[3:4], axis=-1, keepdims=True)) + lam_init)
    o = acc_ref[0] * (1.0 / l_ref[0]) - lam * (acc_ref[1] * (1.0 / l_ref[1]))
    o_ref[...] = (_rms(o.T, g_ref[...]) * (1.0 - lam_init)).astype(o_ref.dtype)


def _fox_kernel(kmax_ref, qt_ref, k_ref, vt_ref, cs_ref, ct_ref, o_ref, m_ref, l_ref, acc_ref):
    n_heads = m_ref.shape[0]
    h0 = pl.program_id(0) * n_heads
    qi = pl.program_id(1)
    tq = qt_ref.shape[1]
    tk = vt_ref.shape[2]
    qt = qt_ref[...]
    ct_all = ct_ref[...]
    head = lax.broadcasted_iota(jnp.int32, ct_all.shape, 0)
    heads = [slice(a * HEAD_DIM, (a + 1) * HEAD_DIM) for a in range(n_heads)]
    ct, q_norm = [], []
    for a in range(n_heads):
        ct.append(jnp.sum(jnp.where(head == h0 + a, ct_all, 0.0), axis=0, keepdims=True))
        qf = qt[heads[a]].astype(F32)
        q_norm.append(jnp.sqrt(jnp.sum(qf * qf, axis=0, keepdims=True)))
    m_ref[...] = jnp.full_like(m_ref, NEG)
    l_ref[...] = jnp.zeros_like(l_ref)
    acc_ref[...] = jnp.zeros_like(acc_ref)

    def scores(j, masked):
        rows = pl.ds(pl.multiple_of(j * tk, tk), tk)
        kb = k_ref[rows, :]
        zs = []
        for a in range(n_heads):
            cs = pltpu.repeat(cs_ref[a, rows, :], tq // LANES, 1)
            z = _dot(kb[:, heads[a]], qt[heads[a]]) + ct[a] - cs
            if masked:
                key, qry = _tile_iotas(tk, tq)
                z = jnp.where(key <= qry, z, NEG)
            zs.append(z)
        return zs

    def consume(zs, j):
        vb = vt_ref[j]
        jn = jnp.maximum(j - 1, 0)
        end = pl.ds(pl.multiple_of((jn + 1) * tk - 8, 8), 8)
        gap = None
        for a, z in enumerate(zs):
            m_old = m_ref[a]
            m_new = jnp.maximum(m_old, jnp.max(z, axis=0, keepdims=True))
            alpha = jnp.exp2(m_old - m_new)
            p = jnp.exp2(z - m_new)
            l_ref[a] = alpha * l_ref[a] + jnp.sum(p, axis=0, keepdims=True)
            acc_ref[a] = alpha * acc_ref[a] + _dot(vb[heads[a]], p.astype(BF16))
            m_ref[a] = m_new
            c_end = pltpu.repeat(cs_ref[a, end, :][7:8, :], tq // LANES, 1)
            bound = q_norm[a] * kmax_ref[jn, h0 + a] + ct[a] - c_end + BOUND_SLACK
            gap = bound - m_new if gap is None else jnp.maximum(gap, bound - m_new)
        return (jnp.max(gap) > -DEAD_LOG2).astype(jnp.int32)

    j_prev = jnp.maximum(qi - 1, 0)
    z_diag = scores(qi, True)
    z_prev = [jnp.where(qi > 0, z, NEG) for z in scores(j_prev, False)]
    consume(z_diag, qi)
    live = consume(z_prev, j_prev)

    def body(carry):
        n, _ = carry
        return n + 1, consume(scores(qi - n, False), qi - n)

    lax.while_loop(lambda c: (c[0] <= qi) & (c[1] > 0), body, (jnp.int32(2), live))
    for a in range(n_heads):
        o_ref[:, heads[a]] = (acc_ref[a] * (1.0 / l_ref[a])).T.astype(o_ref.dtype)


def _attention(kind, qt, k, vt, extra, layer):
    d, s = qt.shape
    nb, _, tk = vt.shape
    tq = tk
    hw = 2 * HEAD_DIM
    n_heads = d // hw
    per_head = _resident if kind == 1 else pl.BlockSpec
    specs = [pl.BlockSpec((hw, tq), lambda h, i: (h, i)),
             per_head((s, hw), lambda h, i: (0, h)),
             per_head((nb, hw, tk), lambda h, i: (0, h, 0))]
    ns = hw // HEAD_DIM
    dv = hw if kind == 1 else HEAD_DIM
    stats = [pltpu.VMEM((ns, 1, tq), F32)] * 2 + [pltpu.VMEM((ns, dv, tq), F32)]
    args = [qt, k, vt]
    if kind == 0:
        kern, vmem = _sb_kernel, 48
        scratch = [pltpu.VMEM((ns, HEAD_DIM, tq), F32), pltpu.VMEM((ns, 1, tq), F32)]
    elif kind == 1:
        lam_init = 0.8 - 0.6 * math.exp(-0.3 * layer)
        kern, vmem = functools.partial(_diff_kernel, lam_init=lam_init), 58
        args += list(extra)
        specs += [pl.BlockSpec((4, HEAD_DIM), lambda h, i: (0, 0)),
                  pl.BlockSpec((1, hw), lambda h, i: (0, 0))]
        scratch = stats + [pltpu.VMEM((ns, tk, tq), F32), pltpu.VMEM((ns, 1, tq), F32)] * 2
    else:
        kern, vmem = _fox_kernel, 58
        cs, ct, kmax = extra
        args = [kmax] + args + [cs, ct]
        specs = [pl.BlockSpec(memory_space=pltpu.SMEM)] + specs
        specs += [_resident((ns, s, LANES), lambda h, i: (h, 0, 0)),
                  pl.BlockSpec((d // HEAD_DIM, tq), lambda h, i: (0, i))]
        scratch = stats
    return pl.pallas_call(
        kern, grid=(n_heads, s // tq), in_specs=specs,
        out_specs=pl.BlockSpec((tq, hw), lambda h, i: (i, h)),
        out_shape=jax.ShapeDtypeStruct((s, d), BF16), scratch_shapes=scratch,
        compiler_params=_params(vmem, 2), name=f"attn{kind}")(*args)


def _prep_ffn(w_gate, w_up, w_down):
    return w_gate.astype(BF16), w_up.astype(BF16), w_down.astype(BF16)


def _rope_tables(s):
    inv = ROPE_THETA ** (-jnp.arange(ROT_HALF, dtype=F32) * 2.0 / ROT_DIM)
    ang = jnp.arange(s).astype(F32)[:, None] * inv[None, :]
    cos, sin = jnp.cos(ang), jnp.sin(ang)
    pad = HEAD_DIM - ROT_DIM
    zeros = jnp.zeros_like(sin)
    cos_f = jnp.concatenate([cos, cos, jnp.ones((s, pad), F32)], axis=1)
    sin_a = jnp.concatenate([-sin, zeros, jnp.zeros((s, pad), F32)], axis=1)
    sin_b = jnp.concatenate([zeros, sin, jnp.zeros((s, pad), F32)], axis=1)
    return cos.T, sin.T, cos_f, sin_a, sin_b


def kernel(x, norm_g, final_g, ffn_w_gate, ffn_w_up, ffn_w_down, w_qkv, w_o,
           diff_lambda, diff_subln_g, fox_w_f, fox_b_f):
    b, s, d = x.shape
    assert b == 1 and s % DENSE_TILE == 0 and d % (2 * HEAD_DIM) == 0
    depth = w_qkv.shape[0]
    n_heads = d // HEAD_DIM
    xs = x.reshape(s, d)
    for i in range(depth):
        kind, j = i % N_MIXERS, i // N_MIXERS
        xs = _ffn(xs, norm_g[i, 0][None], *_prep_ffn(ffn_w_gate[i, 0], ffn_w_up[i, 0], ffn_w_down[i, 0]))
        wq, wk, wv = w_qkv[i][:, :d], w_qkv[i][:, d:2 * d], w_qkv[i][:, 2 * d:]
        rope = fox = None
        if kind == 1:
            rope = _rope_tables(s)
        if kind == 2:
            wf = jnp.zeros((d, LANES), F32).at[:, :n_heads].set(fox_w_f[j]).astype(BF16)
            bf = jnp.zeros((1, LANES), F32).at[0, :n_heads].set(fox_b_f[j])
            fox = (wf, bf)
        outs = _qkv(xs, norm_g[i, 1][None], wq.T.astype(BF16), wk.astype(BF16), wv.T.astype(BF16),
                    kind, DENSE_TILE if kind == 1 else ROW_TILE, rope=rope, fox=fox)
        extra = ()
        if kind == 1:
            extra = (diff_lambda[j], diff_subln_g[j][None])
        if kind == 2:
            extra = (outs[3], outs[4], outs[5].reshape(-1, LANES))
        mix = _attention(kind, outs[0], outs[1], outs[2], extra, i)
        xs = _ffn(xs, norm_g[i, 2][None], *_prep_ffn(ffn_w_gate[i, 1], ffn_w_up[i, 1], ffn_w_down[i, 1]),
                  proj=(mix, w_o[i].astype(BF16)),
                  final_g=final_g[None] if i == depth - 1 else None)
    return xs.reshape(b, s, d)
```

```python
import functools
import math

import jax
import jax.numpy as jnp
from jax import lax
from jax.experimental import pallas as pl
from jax.experimental.pallas import tpu as pltpu

F32 = jnp.float32
BF16 = jnp.bfloat16

HEAD_DIM = 128
N_MIXERS = 3
ROPE_THETA = 500000.0
ROT_DIM = HEAD_DIM // 4
ROT_HALF = ROT_DIM // 2
EPS = 1e-6
NEG = -1e30
SB_NEG = -1e4
LOG2E = 1.4426950408889634
Q_SCALE = HEAD_DIM ** -0.5 * LOG2E

LANES = 128
MXU_DIM = 256
VMEM_BYTES = 64 * 1024 * 1024

ROW_TILE = 512
DENSE_TILE = 1024
FFN_CHUNK = 256
CUM_BLOCK = MXU_DIM
DEAD_LOG2 = 160.0
BOUND_SLACK = 1.0


def _dot(a, b):
    return jnp.dot(a, b, preferred_element_type=F32)


def _dot_nt(a, b):
    return lax.dot_general(a, b, (((1,), (1,)), ((), ())), preferred_element_type=F32)


def _rms(x, g):
    return x * lax.rsqrt(jnp.mean(x * x, axis=-1, keepdims=True) + EPS) * g


def _params(vmem_mib, n_axes):
    return pltpu.CompilerParams(
        dimension_semantics=("arbitrary",) * n_axes,
        vmem_limit_bytes=vmem_mib * 1024 * 1024)


def _resident(shape, index_map):
    return pl.BlockSpec(shape, index_map, pipeline_mode=pl.Buffered(1))


def _ffn_kernel(*refs, has_proj, has_final, n_chunks):
    it = iter(refs)
    x_ref = next(it)
    if has_proj:
        mix_ref, wo_ref = next(it), next(it)
    g_ref, wg_ref, wu_ref, wd_ref = next(it), next(it), next(it), next(it)
    if has_final:
        fg_ref = next(it)
    o_ref, a_ref = next(it), next(it)

    x = x_ref[...]
    if has_proj:
        x = x + _dot(mix_ref[...], wo_ref[...])
    h = _rms(x, g_ref[...]).astype(BF16)
    for c in range(n_chunks):
        cols = slice(c * FFN_CHUNK, (c + 1) * FFN_CHUNK)
        gate = _dot(h, wg_ref[:, cols].astype(BF16))
        up = _dot(h, wu_ref[:, cols].astype(BF16))
        a_ref[:, cols] = (gate * jax.nn.sigmoid(gate) * up).astype(BF16)
    y = x + 0.5 * _dot(a_ref[...], wd_ref[...].astype(BF16))
    if has_final:
        y = _rms(y, fg_ref[...])
    o_ref[...] = y


def _ffn(x, g, wg, wu, wd, proj=None, final_g=None):
    s, d = x.shape
    f = wd.shape[0]
    tm = ROW_TILE
    row = lambda i: (i, 0)
    const = lambda i: (0, 0)
    args, specs = [x], [pl.BlockSpec((tm, d), row)]
    if proj is not None:
        mix, wo = proj
        args += [mix, wo]
        specs += [pl.BlockSpec((tm, d), row), _resident((d, d), const)]
    args += [g, wg, wu, wd]
    specs += [_resident((1, d), const), _resident((d, f), const), _resident((d, f), const),
              _resident((f, d), const)]
    if final_g is not None:
        args.append(final_g)
        specs.append(_resident((1, d), const))
    kern = functools.partial(_ffn_kernel, has_proj=proj is not None, has_final=final_g is not None,
                             n_chunks=f // FFN_CHUNK)
    return pl.pallas_call(
        kern, grid=(s // tm,), in_specs=specs, out_specs=pl.BlockSpec((tm, d), row),
        out_shape=jax.ShapeDtypeStruct((s, d), F32),
        scratch_shapes=[pltpu.VMEM((tm, f), BF16)],
        compiler_params=_params(58, 1), name="ffn")(*args)


def _qkv_kernel(*refs, kind, n_heads):
    it = iter(refs)
    x_ref, g_ref, wqt_ref, wk_ref, wvt_ref = (next(it) for _ in range(5))
    if kind == 1:
        cost_ref, sint_ref, cosf_ref, sina_ref, sinb_ref = (next(it) for _ in range(5))
    if kind == 2:
        wf_ref, bf_ref = next(it), next(it)
    qt_ref, k_ref, vt_ref = next(it), next(it), next(it)
    if kind == 2:
        cs_ref, ct_ref, kmax_ref, carry_ref, kcarry_ref = (next(it) for _ in range(5))

    h = _rms(x_ref[...], g_ref[...]).astype(BF16)
    qt = _dot_nt(wqt_ref[...], h) * Q_SCALE
    k = _dot(h, wk_ref[...])
    vt_ref[...] = _dot_nt(wvt_ref[...], h).astype(BF16)

    if kind != 1:
        qt_ref[...] = qt.astype(BF16)
        k_ref[...] = k.astype(BF16)
    else:
        cos_t, sin_t = cost_ref[...], sint_ref[...]
        cos_f, sin_a, sin_b = cosf_ref[...], sina_ref[...], sinb_ref[...]
        for hh in range(n_heads):
            r0 = hh * HEAD_DIM
            x1, x2 = qt[r0:r0 + ROT_HALF], qt[r0 + ROT_HALF:r0 + ROT_DIM]
            qt_ref[r0:r0 + ROT_HALF, :] = (x1 * cos_t - x2 * sin_t).astype(BF16)
            qt_ref[r0 + ROT_HALF:r0 + ROT_DIM, :] = (x2 * cos_t + x1 * sin_t).astype(BF16)
            qt_ref[r0 + ROT_DIM:r0 + HEAD_DIM, :] = qt[r0 + ROT_DIM:r0 + HEAD_DIM].astype(BF16)
            kh = k[:, r0:r0 + HEAD_DIM]
            rot = (kh * cos_f + pltpu.roll(kh, HEAD_DIM - ROT_HALF, 1) * sin_a
                   + pltpu.roll(kh, ROT_HALF, 1) * sin_b)
            k_ref[:, r0:r0 + HEAD_DIM] = rot.astype(BF16)

    if kind == 2:
        tm = h.shape[0]

        @pl.when(pl.program_id(0) == 0)
        def _():
            carry_ref[...] = jnp.zeros_like(carry_ref)
            kcarry_ref[...] = jnp.zeros_like(kcarry_ref)

        kb = k.astype(BF16).astype(F32)
        lane = lax.broadcasted_iota(jnp.int32, (1, LANES), 1)
        kmax = kcarry_ref[...]
        for hh in range(n_heads):
            kh = kb[:, hh * HEAD_DIM:(hh + 1) * HEAD_DIM]
            n2 = jnp.max(jnp.sum(kh * kh, axis=1, keepdims=True), axis=0, keepdims=True)
            kmax = jnp.where(lane == hh, jnp.maximum(kmax, jnp.sqrt(n2)), kmax)
        kcarry_ref[...] = kmax
        kmax_ref[...] = kmax

        fl = _dot(h, wf_ref[...]) + bf_ref[...]
        lf = jnp.minimum(fl, 0.0) - jnp.log(1.0 + jnp.exp(-jnp.abs(fl)))
        r = lax.broadcasted_iota(jnp.int32, (tm, tm), 0)
        c = lax.broadcasted_iota(jnp.int32, (tm, tm), 1)
        tri = jnp.where(c <= r, 1.0, 0.0).astype(BF16)
        p0 = lf.astype(BF16)
        r1 = lf - p0.astype(F32)
        p1 = r1.astype(BF16)
        p2 = (r1 - p1.astype(F32)).astype(BF16)
        csum = _dot(tri, p0) + _dot(tri, p1) + _dot(tri, p2) + carry_ref[...]
        carry_ref[...] = csum[tm - 1:tm, :]
        c2 = csum * LOG2E
        ct_ref[...] = c2.T[:n_heads, :]
        for hh in range(n_heads):
            cs_ref[hh] = jnp.broadcast_to(c2[:, hh:hh + 1], (tm, LANES))


def _qkv(x, g, wqt, wk, wvt, kind, tm, rope=None, fox=None):
    s, d = x.shape
    nb = s // tm
    n_heads = d // HEAD_DIM
    row = lambda i: (i, 0)
    col = lambda i: (0, i)
    const = lambda i: (0, 0)
    args = [x, g, wqt, wk, wvt]
    specs = [pl.BlockSpec((tm, d), row), _resident((1, d), const),
             _resident((d, d), const), _resident((d, d), const), _resident((d, d), const)]
    out_shape = [jax.ShapeDtypeStruct((d, s), BF16), jax.ShapeDtypeStruct((s, d), BF16),
                 jax.ShapeDtypeStruct((nb, d, tm), BF16)]
    out_specs = [pl.BlockSpec((d, tm), col), pl.BlockSpec((tm, d), row),
                 pl.BlockSpec((None, d, tm), lambda i: (i, 0, 0))]
    scratch = []
    if kind == 1:
        args += list(rope)
        specs += [pl.BlockSpec((ROT_HALF, tm), col)] * 2 + [pl.BlockSpec((tm, HEAD_DIM), row)] * 3
    if kind == 2:
        args += list(fox)
        specs += [_resident((d, LANES), const), _resident((1, LANES), const)]
        out_shape += [jax.ShapeDtypeStruct((n_heads, s, LANES), F32),
                      jax.ShapeDtypeStruct((n_heads, s), F32),
                      jax.ShapeDtypeStruct((nb, 1, LANES), F32)]
        out_specs += [pl.BlockSpec((n_heads, tm, LANES), lambda i: (0, i, 0)),
                      pl.BlockSpec((n_heads, tm), col),
                      pl.BlockSpec((None, 1, LANES), lambda i: (i, 0, 0))]
        scratch = [pltpu.VMEM((1, LANES), F32)] * 2
    kern = functools.partial(_qkv_kernel, kind=kind, n_heads=n_heads)
    return pl.pallas_call(
        kern, grid=(nb,), in_specs=specs, out_specs=out_specs, out_shape=out_shape,
        scratch_shapes=scratch, compiler_params=_params(40 if tm <= ROW_TILE else 56, 1),
        name=f"qkv{kind}")(*args)


def _tile_iotas(tk, tq):
    return (lax.broadcasted_iota(jnp.int32, (tk, tq), 0),
            lax.broadcasted_iota(jnp.int32, (tk, tq), 1))


def _run_steps(step, n_tiles, set0, set1):
    def pair(i, carry):
        step(2 * i, set0, set1)
        step(2 * i + 1, set1, set0)
        return carry

    lax.fori_loop(0, lax.shift_right_logical(n_tiles - 1, 1), pair, 0)
    odd = lax.bitwise_and(n_tiles, 1) == 1

    @pl.when(odd)
    def _():
        step(n_tiles - 1, set0, set1, last=True)

    @pl.when(jnp.logical_not(odd))
    def _():
        step(n_tiles - 2, set0, set1)
        step(n_tiles - 1, set1, set0, last=True)


def _sb_kernel(qt_ref, k_ref, vt_ref, o_ref, acc_ref, run_ref):
    qi = pl.program_id(1)
    tq = qt_ref.shape[1]
    tk = vt_ref.shape[2]
    n_heads = acc_ref.shape[0]
    qt = qt_ref[...]
    d_i, c_i = _tile_iotas(CUM_BLOCK, CUM_BLOCK)
    later = jnp.where(c_i > d_i, 1.0, 0.0).astype(BF16)
    acc_ref[...] = jnp.zeros_like(acc_ref)
    run_ref[...] = jnp.zeros_like(run_ref)
    heads = [slice(a * HEAD_DIM, (a + 1) * HEAD_DIM) for a in range(n_heads)]
    blk = CUM_BLOCK
    assert tq == tk == 2 * blk
    dead = SB_NEG * LOG2E

    def key_block(z, vcols, acc, run):
        sp = jnp.maximum(z, 0.0) + jnp.log(1.0 + jnp.exp2(-jnp.abs(z))) * LOG2E
        after = _dot(later, sp.astype(BF16)) + run
        w = jnp.exp2(z - sp - after).astype(BF16)
        return acc + _dot(vcols, w), run + jnp.sum(sp, axis=0, keepdims=True)

    j_prev = jnp.maximum(qi - 1, 0)
    row0 = pl.multiple_of(qi * tk, tk)
    k_diag = k_ref[pl.ds(row0, tk), :]
    k_prev = k_ref[pl.ds(pl.multiple_of(j_prev * tk + blk, blk), blk), :]
    v_diag, v_prev = vt_ref[qi], vt_ref[j_prev]
    d_i8, c_i8 = _tile_iotas(blk, blk)
    tri = d_i8 < c_i8
    zs = []
    for hd in heads:
        z_lo = _dot(k_diag[:blk, hd], qt[hd])
        z_hi = jnp.where(tri, _dot(k_diag[blk:, hd], qt[hd, blk:]), dead)
        z_pv = jnp.where(qi > 0, _dot(k_prev[:, hd], qt[hd, :blk]), dead)
        zs.append((jnp.where(tri, z_lo[:, :blk], dead), z_lo[:, blk:], z_hi, z_pv))
    low = None
    for a, (z_a0, z_b1, z_b0, z_a1) in enumerate(zs):
        hd = heads[a]
        zero_acc = jnp.zeros((HEAD_DIM, blk), F32)
        zero_run = jnp.zeros((1, blk), F32)
        acc_b, run_b = key_block(z_b0, v_diag[hd, blk:], zero_acc, zero_run)
        acc_b, run_b = key_block(z_b1, v_diag[hd, :blk], acc_b, run_b)
        acc_a, run_a = key_block(z_a0, v_diag[hd, :blk], zero_acc, zero_run)
        acc_a, run_a = key_block(z_a1, v_prev[hd, blk:], acc_a, run_a)
        acc_ref[a] = jnp.concatenate([acc_a, acc_b], axis=1)
        run = jnp.concatenate([run_a, run_b], axis=1)
        run_ref[a] = run
        low = run if low is None else jnp.minimum(low, run)
    live = (jnp.min(low) < DEAD_LOG2).astype(jnp.int32)

    def body(carry):
        n, _ = carry
        j = qi - n
        kb = k_ref[pl.ds(pl.multiple_of(j * tk, tk), tk), :]
        vb = vt_ref[j]
        key, qry = _tile_iotas(tk, tq)
        taken = (n == 1) & (key >= blk) & (qry < blk)
        low = None
        for a, hd in enumerate(heads):
            z = jnp.where(taken, dead, _dot(kb[:, hd], qt[hd]))
            acc, run = acc_ref[a], run_ref[a]
            for b in reversed(range(tk // blk)):
                sl = slice(b * blk, (b + 1) * blk)
                acc, run = key_block(z[sl], vb[hd, sl], acc, run)
            acc_ref[a], run_ref[a] = acc, run
            low = run if low is None else jnp.minimum(low, run)
        return n + 1, (jnp.min(low) < DEAD_LOG2).astype(jnp.int32)

    lax.while_loop(lambda c: (c[0] <= qi) & (c[1] > 0), body, (jnp.int32(1), live))
    for a in range(n_heads):
        o_ref[:, heads[a]] = acc_ref[a].T.astype(o_ref.dtype)


def _softmax_pipeline(qi, score, vt_ref, bufs, m_ref, l_ref, acc_ref):
    set0, set1 = bufs
    n_streams = m_ref.shape[0]
    m_ref[...] = jnp.full_like(m_ref, NEG)
    l_ref[...] = jnp.zeros_like(l_ref)
    acc_ref[...] = jnp.zeros_like(acc_ref)

    def put_scores(j, masked, z_ref, t_ref):
        for a, z in enumerate(score(j, masked)):
            z_ref[a] = z
            t_ref[a] = jnp.max(z, axis=0, keepdims=True)

    def step(n, cur, oth, last=False):
        j = qi - n
        if not last:
            put_scores(j - 1, False, *oth)
        z_ref, t_ref = cur
        vb = vt_ref[j]
        for a in range(n_streams):
            m_old = m_ref[a]
            m_new = jnp.maximum(m_old, t_ref[a])
            alpha = jnp.exp2(m_old - m_new)
            p = jnp.exp2(z_ref[a] - m_new)
            l_ref[a] = alpha * l_ref[a] + jnp.sum(p, axis=0, keepdims=True)
            m_ref[a] = m_new
            acc_ref[a] = alpha * acc_ref[a] + _dot(vb, p.astype(BF16))

    put_scores(qi, True, *set0)
    _run_steps(step, qi + 1, set0, set1)


def _diff_kernel(qt_ref, k_ref, vt_ref, lam_ref, g_ref, o_ref, m_ref, l_ref, acc_ref, *bufs, lam_init):
    qi = pl.program_id(1)
    tq = qt_ref.shape[1]
    tk = vt_ref.shape[2]
    qt = qt_ref[...]

    def scores(j, masked):
        kb = k_ref[pl.ds(pl.multiple_of(j * tk, tk), tk), :]
        zs = []
        for a in range(2):
            sl = slice(a * HEAD_DIM, (a + 1) * HEAD_DIM)
            z = _dot(kb[:, sl], qt[sl])
            if masked:
                key, qry = _tile_iotas(tk, tq)
                z = jnp.where(key <= qry, z, NEG)
            zs.append(z)
        return zs

    _softmax_pipeline(qi, scores, vt_ref, (bufs[:2], bufs[2:]), m_ref, l_ref, acc_ref)
    lp = lam_ref[...]
    lam = (jnp.exp(jnp.sum(lp[0:1] * lp[1:2], axis=-1, keepdims=True))
           - jnp.exp(jnp.sum(lp[2:3] * lp[3:4], axis=-1, keepdims=True)) + lam_init)
    o = acc_ref[0] * (1.0 / l_ref[0]) - lam * (acc_ref[1] * (1.0 / l_ref[1]))
    o_ref[...] = (_rms(o.T, g_ref[...]) * (1.0 - lam_init)).astype(o_ref.dtype)


def _fox_kernel(kmax_ref, qt_ref, k_ref, vt_ref, cs_ref, ct_ref, o_ref, m_ref, l_ref, acc_ref):
    n_heads = m_ref.shape[0]
    h0 = pl.program_id(0) * n_heads
    qi = pl.program_id(1)
    tq = qt_ref.shape[1]
    tk = vt_ref.shape[2]
    qt = qt_ref[...]
    ct_all = ct_ref[...]
    head = lax.broadcasted_iota(jnp.int32, ct_all.shape, 0)
    heads = [slice(a * HEAD_DIM, (a + 1) * HEAD_DIM) for a in range(n_heads)]
    ct, q_norm = [], []
    for a in range(n_heads):
        ct.append(jnp.sum(jnp.where(head == h0 + a, ct_all, 0.0), axis=0, keepdims=True))
        qf = qt[heads[a]].astype(F32)
        q_norm.append(jnp.sqrt(jnp.sum(qf * qf, axis=0, keepdims=True)))
    m_ref[...] = jnp.full_like(m_ref, NEG)
    l_ref[...] = jnp.zeros_like(l_ref)
    acc_ref[...] = jnp.zeros_like(acc_ref)

    def scores(j, masked):
        rows = pl.ds(pl.multiple_of(j * tk, tk), tk)
        kb = k_ref[rows, :]
        zs = []
        for a in range(n_heads):
            cs = pltpu.repeat(cs_ref[a, rows, :], tq // LANES, 1)
            z = _dot(kb[:, heads[a]], qt[heads[a]]) + ct[a] - cs
            if masked:
                key, qry = _tile_iotas(tk, tq)
                z = jnp.where(key <= qry, z, NEG)
            zs.append(z)
        return zs

    def consume(zs, j):
        vb = vt_ref[j]
        jn = jnp.maximum(j - 1, 0)
        end = pl.ds(pl.multiple_of((jn + 1) * tk - 8, 8), 8)
        gap = None
        for a, z in enumerate(zs):
            m_old = m_ref[a]
            m_new = jnp.maximum(m_old, jnp.max(z, axis=0, keepdims=True))
            alpha = jnp.exp2(m_old - m_new)
            p = jnp.exp2(z - m_new)
            l_ref[a] = alpha * l_ref[a] + jnp.sum(p, axis=0, keepdims=True)
            acc_ref[a] = alpha * acc_ref[a] + _dot(vb[heads[a]], p.astype(BF16))
            m_ref[a] = m_new
            c_end = pltpu.repeat(cs_ref[a, end, :][7:8, :], tq // LANES, 1)
            bound = q_norm[a] * kmax_ref[jn, h0 + a] + ct[a] - c_end + BOUND_SLACK
            gap = bound - m_new if gap is None else jnp.maximum(gap, bound - m_new)
        return (jnp.max(gap) > -DEAD_LOG2).astype(jnp.int32)

    j_prev = jnp.maximum(qi - 1, 0)
    z_diag = scores(qi, True)
    z_prev = [jnp.where(qi > 0, z, NEG) for z in scores(j_prev, False)]
    consume(z_diag, qi)
    live = consume(z_prev, j_prev)

    def body(carry):
        n, _ = carry
        return n + 1, consume(scores(qi - n, False), qi - n)

    lax.while_loop(lambda c: (c[0] <= qi) & (c[1] > 0), body, (jnp.int32(2), live))
    for a in range(n_heads):
        o_ref[:, heads[a]] = (acc_ref[a] * (1.0 / l_ref[a])).T.astype(o_ref.dtype)


def _attention(kind, qt, k, vt, extra, layer):
    d, s = qt.shape
    nb, _, tk = vt.shape
    tq = tk
    hw = 2 * HEAD_DIM
    n_heads = d // hw
    per_head = _resident if kind == 1 else pl.BlockSpec
    specs = [pl.BlockSpec((hw, tq), lambda h, i: (h, i)),
             per_head((s, hw), lambda h, i: (0, h)),
             per_head((nb, hw, tk), lambda h, i: (0, h, 0))]
    ns = hw // HEAD_DIM
    dv = hw if kind == 1 else HEAD_DIM
    stats = [pltpu.VMEM((ns, 1, tq), F32)] * 2 + [pltpu.VMEM((ns, dv, tq), F32)]
    args = [qt, k, vt]
    if kind == 0:
        kern, vmem = _sb_kernel, 48
        scratch = [pltpu.VMEM((ns, HEAD_DIM, tq), F32), pltpu.VMEM((ns, 1, tq), F32)]
    elif kind == 1:
        lam_init = 0.8 - 0.6 * math.exp(-0.3 * layer)
        kern, vmem = functools.partial(_diff_kernel, lam_init=lam_init), 58
        args += list(extra)
        specs += [pl.BlockSpec((4, HEAD_DIM), lambda h, i: (0, 0)),
                  pl.BlockSpec((1, hw), lambda h, i: (0, 0))]
        scratch = stats + [pltpu.VMEM((ns, tk, tq), F32), pltpu.VMEM((ns, 1, tq), F32)] * 2
    else:
        kern, vmem = _fox_kernel, 58
        cs, ct, kmax = extra
        args = [kmax] + args + [cs, ct]
        specs = [pl.BlockSpec(memory_space=pltpu.SMEM)] + specs
        specs += [_resident((ns, s, LANES), lambda h, i: (h, 0, 0)),
                  pl.BlockSpec((d // HEAD_DIM, tq), lambda h, i: (0, i))]
        scratch = stats
    return pl.pallas_call(
        kern, grid=(n_heads, s // tq), in_specs=specs,
        out_specs=pl.BlockSpec((tq, hw), lambda h, i: (i, h)),
        out_shape=jax.ShapeDtypeStruct((s, d), BF16), scratch_shapes=scratch,
        compiler_params=_params(vmem, 2), name=f"attn{kind}")(*args)


def _rope_tables(s):
    inv = ROPE_THETA ** (-jnp.arange(ROT_HALF, dtype=F32) * 2.0 / ROT_DIM)
    ang = jnp.arange(s).astype(F32)[:, None] * inv[None, :]
    cos, sin = jnp.cos(ang), jnp.sin(ang)
    pad = HEAD_DIM - ROT_DIM
    zeros = jnp.zeros_like(sin)
    cos_f = jnp.concatenate([cos, cos, jnp.ones((s, pad), F32)], axis=1)
    sin_a = jnp.concatenate([-sin, zeros, jnp.zeros((s, pad), F32)], axis=1)
    sin_b = jnp.concatenate([zeros, sin, jnp.zeros((s, pad), F32)], axis=1)
    return cos.T, sin.T, cos_f, sin_a, sin_b


def kernel(x, norm_g, final_g, ffn_w_gate, ffn_w_up, ffn_w_down, w_qkv, w_o,
           diff_lambda, diff_subln_g, fox_w_f, fox_b_f):
    b, s, d = x.shape
    assert b == 1 and s % DENSE_TILE == 0 and d % (2 * HEAD_DIM) == 0
    depth = w_qkv.shape[0]
    n_heads = d // HEAD_DIM
    xs = x.reshape(s, d)
    for i in range(depth):
        kind, j = i % N_MIXERS, i // N_MIXERS
        xs = _ffn(xs, norm_g[i, 0][None], ffn_w_gate[i, 0], ffn_w_up[i, 0], ffn_w_down[i, 0])
        wq, wk, wv = w_qkv[i][:, :d], w_qkv[i][:, d:2 * d], w_qkv[i][:, 2 * d:]
        rope = fox = None
        if kind == 1:
            rope = _rope_tables(s)
        if kind == 2:
            wf = jnp.zeros((d, LANES), F32).at[:, :n_heads].set(fox_w_f[j]).astype(BF16)
            bf = jnp.zeros((1, LANES), F32).at[0, :n_heads].set(fox_b_f[j])
            fox = (wf, bf)
        outs = _qkv(xs, norm_g[i, 1][None], wq.T.astype(BF16), wk.astype(BF16), wv.T.astype(BF16),
                    kind, DENSE_TILE if kind == 1 else ROW_TILE, rope=rope, fox=fox)
        extra = ()
        if kind == 1:
            extra = (diff_lambda[j], diff_subln_g[j][None])
        if kind == 2:
            extra = (outs[3], outs[4], outs[5].reshape(-1, LANES))
        mix = _attention(kind, outs[0], outs[1], outs[2], extra, i)
        xs = _ffn(xs, norm_g[i, 2][None], ffn_w_gate[i, 1], ffn_w_up[i, 1], ffn_w_down[i, 1],
                  proj=(mix, w_o[i].astype(BF16)),
                  final_g=final_g[None] if i == depth - 1 else None)
    return xs.reshape(b, s, d)
```

```python
import functools
import math

import jax
import jax.numpy as jnp
from jax import lax
from jax.experimental import pallas as pl
from jax.experimental.pallas import tpu as pltpu

F32 = jnp.float32
BF16 = jnp.bfloat16

HEAD_DIM = 128
N_MIXERS = 3
ROPE_THETA = 500000.0
ROT_DIM = HEAD_DIM // 4
ROT_HALF = ROT_DIM // 2
EPS = 1e-6
NEG = -1e30
SB_NEG = -1e4
LOG2E = 1.4426950408889634
Q_SCALE = HEAD_DIM ** -0.5 * LOG2E

LANES = 128
MXU_DIM = 256
VMEM_BYTES = 64 * 1024 * 1024

ROW_TILE = 512
DENSE_TILE = 1024
FFN_CHUNK = 256
CUM_BLOCK = MXU_DIM
DEAD_LOG2 = 160.0
BOUND_SLACK = 1.0


def _dot(a, b):
    return jnp.dot(a, b, preferred_element_type=F32)


def _dot_tn(a, b):
    return lax.dot_general(a, b, (((0,), (1,)), ((), ())), preferred_element_type=F32)


def _rms(x, g):
    return x * lax.rsqrt(jnp.mean(x * x, axis=-1, keepdims=True) + EPS) * g


def _params(vmem_mib, n_axes):
    return pltpu.CompilerParams(
        dimension_semantics=("arbitrary",) * n_axes,
        vmem_limit_bytes=vmem_mib * 1024 * 1024)


def _resident(shape, index_map):
    return pl.BlockSpec(shape, index_map, pipeline_mode=pl.Buffered(1))


def _ffn_kernel(*refs, has_proj, has_final, n_chunks):
    it = iter(refs)
    x_ref = next(it)
    if has_proj:
        mix_ref, wo_ref = next(it), next(it)
    g_ref, wg_ref, wu_ref, wd_ref = next(it), next(it), next(it), next(it)
    if has_final:
        fg_ref = next(it)
    o_ref, a_ref = next(it), next(it)

    x = x_ref[...]
    if has_proj:
        x = x + _dot(mix_ref[...], wo_ref[...].astype(BF16))
    h = _rms(x, g_ref[...]).astype(BF16)
    for c in range(n_chunks):
        cols = slice(c * FFN_CHUNK, (c + 1) * FFN_CHUNK)
        gate = _dot(h, wg_ref[:, cols].astype(BF16))
        up = _dot(h, wu_ref[:, cols].astype(BF16))
        a_ref[:, cols] = (gate * jax.nn.sigmoid(gate) * up).astype(BF16)
    y = x + 0.5 * _dot(a_ref[...], wd_ref[...].astype(BF16))
    if has_final:
        y = _rms(y, fg_ref[...])
    o_ref[...] = y


def _ffn(x, g, wg, wu, wd, layer, which, proj=None, final_g=None):
    s, d = x.shape
    f = wd.shape[-2]
    tm = ROW_TILE
    row = lambda i: (i, 0)
    const = lambda i: (0, 0)
    pick = lambda i: (layer, which, 0, 0)
    args, specs = [x], [pl.BlockSpec((tm, d), row)]
    if proj is not None:
        mix, wo = proj
        args += [mix, wo]
        specs += [pl.BlockSpec((tm, d), row), _resident((None, d, d), lambda i: (layer, 0, 0))]
    args += [g, wg, wu, wd]
    specs += [_resident((1, d), const), _resident((None, None, d, f), pick),
              _resident((None, None, d, f), pick), _resident((None, None, f, d), pick)]
    if final_g is not None:
        args.append(final_g)
        specs.append(_resident((1, d), const))
    kern = functools.partial(_ffn_kernel, has_proj=proj is not None, has_final=final_g is not None,
                             n_chunks=f // FFN_CHUNK)
    return pl.pallas_call(
        kern, grid=(s // tm,), in_specs=specs, out_specs=pl.BlockSpec((tm, d), row),
        out_shape=jax.ShapeDtypeStruct((s, d), F32),
        scratch_shapes=[pltpu.VMEM((tm, f), BF16)],
        compiler_params=_params(58, 1), name="ffn")(*args)


def _qkv_kernel(*refs, kind, n_heads):
    it = iter(refs)
    x_ref, g_ref, wq_ref, wk_ref, wv_ref = (next(it) for _ in range(5))
    if kind == 1:
        cost_ref, sint_ref, cosf_ref, sina_ref, sinb_ref = (next(it) for _ in range(5))
    if kind == 2:
        wf_ref, bf_ref = next(it), next(it)
    qt_ref, k_ref, vt_ref = next(it), next(it), next(it)
    if kind == 2:
        cs_ref, ct_ref, kmax_ref, carry_ref, kcarry_ref = (next(it) for _ in range(5))

    h = _rms(x_ref[...], g_ref[...]).astype(BF16)
    qt = _dot_tn(wq_ref[...].astype(BF16), h) * Q_SCALE
    k = _dot(h, wk_ref[...].astype(BF16))
    vt_ref[...] = _dot_tn(wv_ref[...].astype(BF16), h).astype(BF16)

    if kind != 1:
        qt_ref[...] = qt.astype(BF16)
        k_ref[...] = k.astype(BF16)
    else:
        cos_t, sin_t = cost_ref[...], sint_ref[...]
        cos_f, sin_a, sin_b = cosf_ref[...], sina_ref[...], sinb_ref[...]
        for hh in range(n_heads):
            r0 = hh * HEAD_DIM
            x1, x2 = qt[r0:r0 + ROT_HALF], qt[r0 + ROT_HALF:r0 + ROT_DIM]
            qt_ref[r0:r0 + ROT_HALF, :] = (x1 * cos_t - x2 * sin_t).astype(BF16)
            qt_ref[r0 + ROT_HALF:r0 + ROT_DIM, :] = (x2 * cos_t + x1 * sin_t).astype(BF16)
            qt_ref[r0 + ROT_DIM:r0 + HEAD_DIM, :] = qt[r0 + ROT_DIM:r0 + HEAD_DIM].astype(BF16)
            kh = k[:, r0:r0 + HEAD_DIM]
            rot = (kh * cos_f + pltpu.roll(kh, HEAD_DIM - ROT_HALF, 1) * sin_a
                   + pltpu.roll(kh, ROT_HALF, 1) * sin_b)
            k_ref[:, r0:r0 + HEAD_DIM] = rot.astype(BF16)

    if kind == 2:
        tm = h.shape[0]

        @pl.when(pl.program_id(0) == 0)
        def _():
            carry_ref[...] = jnp.zeros_like(carry_ref)
            kcarry_ref[...] = jnp.zeros_like(kcarry_ref)

        kb = k.astype(BF16).astype(F32)
        lane = lax.broadcasted_iota(jnp.int32, (1, LANES), 1)
        kmax = kcarry_ref[...]
        for hh in range(n_heads):
            kh = kb[:, hh * HEAD_DIM:(hh + 1) * HEAD_DIM]
            n2 = jnp.max(jnp.sum(kh * kh, axis=1, keepdims=True), axis=0, keepdims=True)
            kmax = jnp.where(lane == hh, jnp.maximum(kmax, jnp.sqrt(n2)), kmax)
        kcarry_ref[...] = kmax
        kmax_ref[...] = kmax

        fl = _dot(h, wf_ref[...]) + bf_ref[...]
        lf = jnp.minimum(fl, 0.0) - jnp.log(1.0 + jnp.exp(-jnp.abs(fl)))
        r = lax.broadcasted_iota(jnp.int32, (tm, tm), 0)
        c = lax.broadcasted_iota(jnp.int32, (tm, tm), 1)
        tri = jnp.where(c <= r, 1.0, 0.0).astype(BF16)
        p0 = lf.astype(BF16)
        r1 = lf - p0.astype(F32)
        p1 = r1.astype(BF16)
        p2 = (r1 - p1.astype(F32)).astype(BF16)
        csum = _dot(tri, p0) + _dot(tri, p1) + _dot(tri, p2) + carry_ref[...]
        carry_ref[...] = csum[tm - 1:tm, :]
        c2 = csum * LOG2E
        ct_ref[...] = c2.T[:n_heads, :]
        for hh in range(n_heads):
            cs_ref[hh] = jnp.broadcast_to(c2[:, hh:hh + 1], (tm, LANES))


def _qkv(x, g, w_qkv, layer, kind, tm, rope=None, fox=None):
    s, d = x.shape
    nb = s // tm
    n_heads = d // HEAD_DIM
    row = lambda i: (i, 0)
    col = lambda i: (0, i)
    const = lambda i: (0, 0)
    args = [x, g, w_qkv, w_qkv, w_qkv]
    specs = [pl.BlockSpec((tm, d), row), _resident((1, d), const)]
    specs += [_resident((None, d, d), lambda i, c=c: (layer, 0, c)) for c in range(3)]
    out_shape = [jax.ShapeDtypeStruct((d, s), BF16), jax.ShapeDtypeStruct((s, d), BF16),
                 jax.ShapeDtypeStruct((nb, d, tm), BF16)]
    out_specs = [pl.BlockSpec((d, tm), col), pl.BlockSpec((tm, d), row),
                 pl.BlockSpec((None, d, tm), lambda i: (i, 0, 0))]
    scratch = []
    if kind == 1:
        args += list(rope)
        specs += [pl.BlockSpec((ROT_HALF, tm), col)] * 2 + [pl.BlockSpec((tm, HEAD_DIM), row)] * 3
    if kind == 2:
        args += list(fox)
        specs += [_resident((d, LANES), const), _resident((1, LANES), const)]
        out_shape += [jax.ShapeDtypeStruct((n_heads, s, LANES), F32),
                      jax.ShapeDtypeStruct((n_heads, s), F32),
                      jax.ShapeDtypeStruct((nb, 1, LANES), F32)]
        out_specs += [pl.BlockSpec((n_heads, tm, LANES), lambda i: (0, i, 0)),
                      pl.BlockSpec((n_heads, tm), col),
                      pl.BlockSpec((None, 1, LANES), lambda i: (i, 0, 0))]
        scratch = [pltpu.VMEM((1, LANES), F32)] * 2
    kern = functools.partial(_qkv_kernel, kind=kind, n_heads=n_heads)
    return pl.pallas_call(
        kern, grid=(nb,), in_specs=specs, out_specs=out_specs, out_shape=out_shape,
        scratch_shapes=scratch, compiler_params=_params(40 if tm <= ROW_TILE else 56, 1),
        name=f"qkv{kind}")(*args)


def _tile_iotas(tk, tq):
    return (lax.broadcasted_iota(jnp.int32, (tk, tq), 0),
            lax.broadcasted_iota(jnp.int32, (tk, tq), 1))


def _run_steps(step, n_tiles, set0, set1):
    def pair(i, carry):
        step(2 * i, set0, set1)
        step(2 * i + 1, set1, set0)
        return carry

    lax.fori_loop(0, lax.shift_right_logical(n_tiles - 1, 1), pair, 0)
    odd = lax.bitwise_and(n_tiles, 1) == 1

    @pl.when(odd)
    def _():
        step(n_tiles - 1, set0, set1, last=True)

    @pl.when(jnp.logical_not(odd))
    def _():
        step(n_tiles - 2, set0, set1)
        step(n_tiles - 1, set1, set0, last=True)


def _sb_kernel(qt_ref, k_ref, vt_ref, o_ref, acc_ref, run_ref):
    qi = pl.program_id(1)
    tq = qt_ref.shape[1]
    tk = vt_ref.shape[2]
    n_heads = acc_ref.shape[0]
    qt = qt_ref[...]
    d_i, c_i = _tile_iotas(CUM_BLOCK, CUM_BLOCK)
    later = jnp.where(c_i > d_i, 1.0, 0.0).astype(BF16)
    acc_ref[...] = jnp.zeros_like(acc_ref)
    run_ref[...] = jnp.zeros_like(run_ref)
    heads = [slice(a * HEAD_DIM, (a + 1) * HEAD_DIM) for a in range(n_heads)]
    blk = CUM_BLOCK
    assert tq == tk == 2 * blk
    dead = SB_NEG * LOG2E

    def key_block(z, vcols, acc, run):
        sp = jnp.maximum(z, 0.0) + jnp.log(1.0 + jnp.exp2(-jnp.abs(z))) * LOG2E
        after = _dot(later, sp.astype(BF16)) + run
        w = jnp.exp2(z - sp - after).astype(BF16)
        return acc + _dot(vcols, w), run + jnp.sum(sp, axis=0, keepdims=True)

    j_prev = jnp.maximum(qi - 1, 0)
    row0 = pl.multiple_of(qi * tk, tk)
    k_diag = k_ref[pl.ds(row0, tk), :]
    k_prev = k_ref[pl.ds(pl.multiple_of(j_prev * tk + blk, blk), blk), :]
    v_diag, v_prev = vt_ref[qi], vt_ref[j_prev]
    d_i8, c_i8 = _tile_iotas(blk, blk)
    tri = d_i8 < c_i8
    zs = []
    for hd in heads:
        z_lo = _dot(k_diag[:blk, hd], qt[hd])
        z_hi = jnp.where(tri, _dot(k_diag[blk:, hd], qt[hd, blk:]), dead)
        z_pv = jnp.where(qi > 0, _dot(k_prev[:, hd], qt[hd, :blk]), dead)
        zs.append((jnp.where(tri, z_lo[:, :blk], dead), z_lo[:, blk:], z_hi, z_pv))
    low = None
    for a, (z_a0, z_b1, z_b0, z_a1) in enumerate(zs):
        hd = heads[a]
        zero_acc = jnp.zeros((HEAD_DIM, blk), F32)
        zero_run = jnp.zeros((1, blk), F32)
        acc_b, run_b = key_block(z_b0, v_diag[hd, blk:], zero_acc, zero_run)
        acc_b, run_b = key_block(z_b1, v_diag[hd, :blk], acc_b, run_b)
        acc_a, run_a = key_block(z_a0, v_diag[hd, :blk], zero_acc, zero_run)
        acc_a, run_a = key_block(z_a1, v_prev[hd, blk:], acc_a, run_a)
        acc_ref[a] = jnp.concatenate([acc_a, acc_b], axis=1)
        run = jnp.concatenate([run_a, run_b], axis=1)
        run_ref[a] = run
        low = run if low is None else jnp.minimum(low, run)
    live = (jnp.min(low) < DEAD_LOG2).astype(jnp.int32)

    def body(carry):
        n, _ = carry
        j = qi - n
        kb = k_ref[pl.ds(pl.multiple_of(j * tk, tk), tk), :]
        vb = vt_ref[j]
        key, qry = _tile_iotas(tk, tq)
        taken = (n == 1) & (key >= blk) & (qry < blk)
        low = None
        for a, hd in enumerate(heads):
            z = jnp.where(taken, dead, _dot(kb[:, hd], qt[hd]))
            acc, run = acc_ref[a], run_ref[a]
            for b in reversed(range(tk // blk)):
                sl = slice(b * blk, (b + 1) * blk)
                acc, run = key_block(z[sl], vb[hd, sl], acc, run)
            acc_ref[a], run_ref[a] = acc, run
            low = run if low is None else jnp.minimum(low, run)
        return n + 1, (jnp.min(low) < DEAD_LOG2).astype(jnp.int32)

    lax.while_loop(lambda c: (c[0] <= qi) & (c[1] > 0), body, (jnp.int32(1), live))
    for a in range(n_heads):
        o_ref[:, heads[a]] = acc_ref[a].T.astype(o_ref.dtype)


def _softmax_pipeline(qi, score, vt_ref, bufs, m_ref, l_ref, acc_ref):
    set0, set1 = bufs
    n_streams = m_ref.shape[0]
    m_ref[...] = jnp.full_like(m_ref, NEG)
    l_ref[...] = jnp.zeros_like(l_ref)
    acc_ref[...] = jnp.zeros_like(acc_ref)

    def put_scores(j, masked, z_ref, t_ref):
        for a, z in enumerate(score(j, masked)):
            z_ref[a] = z
            t_ref[a] = jnp.max(z, axis=0, keepdims=True)

    def step(n, cur, oth, last=False):
        j = qi - n
        if not last:
            put_scores(j - 1, False, *oth)
        z_ref, t_ref = cur
        vb = vt_ref[j]
        for a in range(n_streams):
            m_old = m_ref[a]
            m_new = jnp.maximum(m_old, t_ref[a])
            alpha = jnp.exp2(m_old - m_new)
            p = jnp.exp2(z_ref[a] - m_new)
            l_ref[a] = alpha * l_ref[a] + jnp.sum(p, axis=0, keepdims=True)
            m_ref[a] = m_new
            acc_ref[a] = alpha * acc_ref[a] + _dot(vb, p.astype(BF16))

    put_scores(qi, True, *set0)
    _run_steps(step, qi + 1, set0, set1)


def _diff_kernel(qt_ref, k_ref, vt_ref, lam_ref, g_ref, o_ref, m_ref, l_ref, acc_ref, *bufs, lam_init):
    qi = pl.program_id(1)
    tq = qt_ref.shape[1]
    tk = vt_ref.shape[2]
    qt = qt_ref[...]

    def scores(j, masked):
        kb = k_ref[pl.ds(pl.multiple_of(j * tk, tk), tk), :]
        zs = []
        for a in range(2):
            sl = slice(a * HEAD_DIM, (a + 1) * HEAD_DIM)
            z = _dot(kb[:, sl], qt[sl])
            if masked:
                key, qry = _tile_iotas(tk, tq)
                z = jnp.where(key <= qry, z, NEG)
            zs.append(z)
        return zs

    _softmax_pipeline(qi, scores, vt_ref, (bufs[:2], bufs[2:]), m_ref, l_ref, acc_ref)
    lp = lam_ref[...]
    lam = (jnp.exp(jnp.sum(lp[0:1] * lp[1:2], axis=-1, keepdims=True))
           - jnp.exp(jnp.sum(lp[2:3] * lp[3:4], axis=-1, keepdims=True)) + lam_init)
    o = acc_ref[0] * (1.0 / l_ref[0]) - lam * (acc_ref[1] * (1.0 / l_ref[1]))
    o_ref[...] = (_rms(o.T, g_ref[...]) * (1.0 - lam_init)).astype(o_ref.dtype)


def _fox_kernel(kmax_ref, qt_ref, k_ref, vt_ref, cs_ref, ct_ref, o_ref, m_ref, l_ref, acc_ref):
    n_heads = m_ref.shape[0]
    h0 = pl.program_id(0) * n_heads
    qi = pl.program_id(1)
    tq = qt_ref.shape[1]
    tk = vt_ref.shape[2]
    qt = qt_ref[...]
    ct_all = ct_ref[...]
    head = lax.broadcasted_iota(jnp.int32, ct_all.shape, 0)
    heads = [slice(a * HEAD_DIM, (a + 1) * HEAD_DIM) for a in range(n_heads)]
    ct, q_norm = [], []
    for a in range(n_heads):
        ct.append(jnp.sum(jnp.where(head == h0 + a, ct_all, 0.0), axis=0, keepdims=True))
        qf = qt[heads[a]].astype(F32)
        q_norm.append(jnp.sqrt(jnp.sum(qf * qf, axis=0, keepdims=True)))
    m_ref[...] = jnp.full_like(m_ref, NEG)
    l_ref[...] = jnp.zeros_like(l_ref)
    acc_ref[...] = jnp.zeros_like(acc_ref)

    def scores(j, masked):
        rows = pl.ds(pl.multiple_of(j * tk, tk), tk)
        kb = k_ref[rows, :]
        zs = []
        for a in range(n_heads):
            cs = pltpu.repeat(cs_ref[a, rows, :], tq // LANES, 1)
            z = _dot(kb[:, heads[a]], qt[heads[a]]) + ct[a] - cs
            if masked:
                key, qry = _tile_iotas(tk, tq)
                z = jnp.where(key <= qry, z, NEG)
            zs.append(z)
        return zs

    def consume(zs, j):
        vb = vt_ref[j]
        jn = jnp.maximum(j - 1, 0)
        end = pl.ds(pl.multiple_of((jn + 1) * tk - 8, 8), 8)
        gap = None
        for a, z in enumerate(zs):
            m_old = m_ref[a]
            m_new = jnp.maximum(m_old, jnp.max(z, axis=0, keepdims=True))
            alpha = jnp.exp2(m_old - m_new)
            p = jnp.exp2(z - m_new)
            l_ref[a] = alpha * l_ref[a] + jnp.sum(p, axis=0, keepdims=True)
            acc_ref[a] = alpha * acc_ref[a] + _dot(vb[heads[a]], p.astype(BF16))
            m_ref[a] = m_new
            c_end = pltpu.repeat(cs_ref[a, end, :][7:8, :], tq // LANES, 1)
            bound = q_norm[a] * kmax_ref[jn, h0 + a] + ct[a] - c_end + BOUND_SLACK
            gap = bound - m_new if gap is None else jnp.maximum(gap, bound - m_new)
        return (jnp.max(gap) > -DEAD_LOG2).astype(jnp.int32)

    j_prev = jnp.maximum(qi - 1, 0)
    z_diag = scores(qi, True)
    z_prev = [jnp.where(qi > 0, z, NEG) for z in scores(j_prev, False)]
    consume(z_diag, qi)
    live = consume(z_prev, j_prev)

    def body(carry):
        n, _ = carry
        return n + 1, consume(scores(qi - n, False), qi - n)

    lax.while_loop(lambda c: (c[0] <= qi) & (c[1] > 0), body, (jnp.int32(2), live))
    for a in range(n_heads):
        o_ref[:, heads[a]] = (acc_ref[a] * (1.0 / l_ref[a])).T.astype(o_ref.dtype)


def _attention(kind, qt, k, vt, extra, layer):
    d, s = qt.shape
    nb, _, tk = vt.shape
    tq = tk
    hw = 2 * HEAD_DIM
    n_heads = d // hw
    per_head = _resident if kind == 1 else pl.BlockSpec
    specs = [pl.BlockSpec((hw, tq), lambda h, i: (h, i)),
             per_head((s, hw), lambda h, i: (0, h)),
             per_head((nb, hw, tk), lambda h, i: (0, h, 0))]
    ns = hw // HEAD_DIM
    dv = hw if kind == 1 else HEAD_DIM
    stats = [pltpu.VMEM((ns, 1, tq), F32)] * 2 + [pltpu.VMEM((ns, dv, tq), F32)]
    args = [qt, k, vt]
    if kind == 0:
        kern, vmem = _sb_kernel, 48
        scratch = [pltpu.VMEM((ns, HEAD_DIM, tq), F32), pltpu.VMEM((ns, 1, tq), F32)]
    elif kind == 1:
        lam_init = 0.8 - 0.6 * math.exp(-0.3 * layer)
        kern, vmem = functools.partial(_diff_kernel, lam_init=lam_init), 58
        args += list(extra)
        specs += [pl.BlockSpec((4, HEAD_DIM), lambda h, i: (0, 0)),
                  pl.BlockSpec((1, hw), lambda h, i: (0, 0))]
        scratch = stats + [pltpu.VMEM((ns, tk, tq), F32), pltpu.VMEM((ns, 1, tq), F32)] * 2
    else:
        kern, vmem = _fox_kernel, 58
        cs, ct, kmax = extra
        args = [kmax] + args + [cs, ct]
        specs = [pl.BlockSpec(memory_space=pltpu.SMEM)] + specs
        specs += [_resident((ns, s, LANES), lambda h, i: (h, 0, 0)),
                  pl.BlockSpec((d // HEAD_DIM, tq), lambda h, i: (0, i))]
        scratch = stats
    return pl.pallas_call(
        kern, grid=(n_heads, s // tq), in_specs=specs,
        out_specs=pl.BlockSpec((tq, hw), lambda h, i: (i, h)),
        out_shape=jax.ShapeDtypeStruct((s, d), BF16), scratch_shapes=scratch,
        compiler_params=_params(vmem, 2), name=f"attn{kind}")(*args)


def _rope_tables(s):
    inv = ROPE_THETA ** (-jnp.arange(ROT_HALF, dtype=F32) * 2.0 / ROT_DIM)
    ang = jnp.arange(s).astype(F32)[:, None] * inv[None, :]
    cos, sin = jnp.cos(ang), jnp.sin(ang)
    pad = HEAD_DIM - ROT_DIM
    zeros = jnp.zeros_like(sin)
    cos_f = jnp.concatenate([cos, cos, jnp.ones((s, pad), F32)], axis=1)
    sin_a = jnp.concatenate([-sin, zeros, jnp.zeros((s, pad), F32)], axis=1)
    sin_b = jnp.concatenate([zeros, sin, jnp.zeros((s, pad), F32)], axis=1)
    return cos.T, sin.T, cos_f, sin_a, sin_b


def kernel(x, norm_g, final_g, ffn_w_gate, ffn_w_up, ffn_w_down, w_qkv, w_o,
           diff_lambda, diff_subln_g, fox_w_f, fox_b_f):
    b, s, d = x.shape
    assert b == 1 and s % DENSE_TILE == 0 and d % (2 * HEAD_DIM) == 0
    depth = w_qkv.shape[0]
    n_heads = d // HEAD_DIM
    xs = x.reshape(s, d)
    for i in range(depth):
        kind, j = i % N_MIXERS, i // N_MIXERS
        xs = _ffn(xs, norm_g[i, 0][None], ffn_w_gate, ffn_w_up, ffn_w_down, i, 0)
        rope = fox = None
        if kind == 1:
            rope = _rope_tables(s)
        if kind == 2:
            wf = jnp.zeros((d, LANES), F32).at[:, :n_heads].set(fox_w_f[j]).astype(BF16)
            bf = jnp.zeros((1, LANES), F32).at[0, :n_heads].set(fox_b_f[j])
            fox = (wf, bf)
        outs = _qkv(xs, norm_g[i, 1][None], w_qkv, i, kind,
                    DENSE_TILE if kind == 1 else ROW_TILE, rope=rope, fox=fox)
        extra = ()
        if kind == 1:
            extra = (diff_lambda[j], diff_subln_g[j][None])
        if kind == 2:
            extra = (outs[3], outs[4], outs[5].reshape(-1, LANES))
        mix = _attention(kind, outs[0], outs[1], outs[2], extra, i)
        xs = _ffn(xs, norm_g[i, 2][None], ffn_w_gate, ffn_w_up, ffn_w_down, i, 1,
                  proj=(mix, w_o),
                  final_g=final_g[None] if i == depth - 1 else None)
    return xs.reshape(b, s, d)
```

```python
import functools
import math

import jax
import jax.numpy as jnp
from jax import lax
from jax.experimental import pallas as pl
from jax.experimental.pallas import tpu as pltpu

F32 = jnp.float32
BF16 = jnp.bfloat16

HEAD_DIM = 128
N_MIXERS = 3
ROPE_THETA = 500000.0
ROT_DIM = HEAD_DIM // 4
ROT_HALF = ROT_DIM // 2
EPS = 1e-6
NEG = -1e30
SB_NEG = -1e4
LOG2E = 1.4426950408889634
Q_SCALE = HEAD_DIM ** -0.5 * LOG2E

LANES = 128
MXU_DIM = 256
VMEM_BYTES = 64 * 1024 * 1024

ROW_TILE = 512
DENSE_TILE = 1024
SB_HEADS = 4
FFN_CHUNK = 256
CUM_BLOCK = MXU_DIM
DEAD_LOG2 = 160.0
BOUND_SLACK = 1.0


def _dot(a, b):
    return jnp.dot(a, b, preferred_element_type=F32)


def _dot_tn(a, b):
    return lax.dot_general(a, b, (((0,), (1,)), ((), ())), preferred_element_type=F32)


def _rms(x, g):
    return x * lax.rsqrt(jnp.mean(x * x, axis=-1, keepdims=True) + EPS) * g


def _params(vmem_mib, n_axes):
    return pltpu.CompilerParams(
        dimension_semantics=("arbitrary",) * n_axes,
        vmem_limit_bytes=vmem_mib * 1024 * 1024)


def _resident(shape, index_map):
    return pl.BlockSpec(shape, index_map, pipeline_mode=pl.Buffered(1))


def _ffn_kernel(*refs, has_proj, has_final, n_chunks):
    it = iter(refs)
    x_ref = next(it)
    if has_proj:
        mix_ref, wo_ref = next(it), next(it)
    g_ref, wg_ref, wu_ref, wd_ref = next(it), next(it), next(it), next(it)
    if has_final:
        fg_ref = next(it)
    o_ref, a_ref = next(it), next(it)

    x = x_ref[...]
    if has_proj:
        x = x + _dot(mix_ref[...], wo_ref[...].astype(BF16))
    h = _rms(x, g_ref[...]).astype(BF16)
    for c in range(n_chunks):
        cols = slice(c * FFN_CHUNK, (c + 1) * FFN_CHUNK)
        gate = _dot(h, wg_ref[:, cols].astype(BF16))
        up = _dot(h, wu_ref[:, cols].astype(BF16))
        a_ref[:, cols] = (gate * jax.nn.sigmoid(gate) * up).astype(BF16)
    y = x + 0.5 * _dot(a_ref[...], wd_ref[...].astype(BF16))
    if has_final:
        y = _rms(y, fg_ref[...])
    o_ref[...] = y


def _ffn(x, g, wg, wu, wd, layer, which, proj=None, final_g=None):
    s, d = x.shape
    f = wd.shape[-2]
    tm = ROW_TILE
    row = lambda i: (i, 0)
    const = lambda i: (0, 0)
    pick = lambda i: (layer, which, 0, 0)
    args, specs = [x], [pl.BlockSpec((tm, d), row)]
    if proj is not None:
        mix, wo = proj
        args += [mix, wo]
        specs += [pl.BlockSpec((tm, d), row), _resident((None, d, d), lambda i: (layer, 0, 0))]
    args += [g, wg, wu, wd]
    specs += [_resident((1, d), const), _resident((None, None, d, f), pick),
              _resident((None, None, d, f), pick), _resident((None, None, f, d), pick)]
    if final_g is not None:
        args.append(final_g)
        specs.append(_resident((1, d), const))
    kern = functools.partial(_ffn_kernel, has_proj=proj is not None, has_final=final_g is not None,
                             n_chunks=f // FFN_CHUNK)
    return pl.pallas_call(
        kern, grid=(s // tm,), in_specs=specs, out_specs=pl.BlockSpec((tm, d), row),
        out_shape=jax.ShapeDtypeStruct((s, d), F32),
        scratch_shapes=[pltpu.VMEM((tm, f), BF16)],
        compiler_params=_params(58, 1), name="ffn")(*args)


def _qkv_kernel(*refs, kind, n_heads):
    it = iter(refs)
    x_ref, g_ref, wq_ref, wk_ref, wv_ref = (next(it) for _ in range(5))
    if kind == 1:
        cost_ref, sint_ref, cosf_ref, sina_ref, sinb_ref = (next(it) for _ in range(5))
    if kind == 2:
        wf_ref, bf_ref = next(it), next(it)
    qt_ref, k_ref, vt_ref = next(it), next(it), next(it)
    if kind == 2:
        cs_ref, ct_ref, kmax_ref, carry_ref, kcarry_ref = (next(it) for _ in range(5))

    h = _rms(x_ref[...], g_ref[...]).astype(BF16)
    qt = _dot_tn(wq_ref[...].astype(BF16), h) * Q_SCALE
    k = _dot(h, wk_ref[...].astype(BF16))
    vt_ref[...] = _dot_tn(wv_ref[...].astype(BF16), h).astype(BF16)

    if kind != 1:
        qt_ref[...] = qt.astype(BF16)
        k_ref[...] = k.astype(BF16)
    else:
        cos_t, sin_t = cost_ref[...], sint_ref[...]
        cos_f, sin_a, sin_b = cosf_ref[...], sina_ref[...], sinb_ref[...]
        for hh in range(n_heads):
            r0 = hh * HEAD_DIM
            x1, x2 = qt[r0:r0 + ROT_HALF], qt[r0 + ROT_HALF:r0 + ROT_DIM]
            qt_ref[r0:r0 + ROT_HALF, :] = (x1 * cos_t - x2 * sin_t).astype(BF16)
            qt_ref[r0 + ROT_HALF:r0 + ROT_DIM, :] = (x2 * cos_t + x1 * sin_t).astype(BF16)
            qt_ref[r0 + ROT_DIM:r0 + HEAD_DIM, :] = qt[r0 + ROT_DIM:r0 + HEAD_DIM].astype(BF16)
            kh = k[:, r0:r0 + HEAD_DIM]
            rot = (kh * cos_f + pltpu.roll(kh, HEAD_DIM - ROT_HALF, 1) * sin_a
                   + pltpu.roll(kh, ROT_HALF, 1) * sin_b)
            k_ref[:, r0:r0 + HEAD_DIM] = rot.astype(BF16)

    if kind == 2:
        tm = h.shape[0]

        @pl.when(pl.program_id(0) == 0)
        def _():
            carry_ref[...] = jnp.zeros_like(carry_ref)
            kcarry_ref[...] = jnp.zeros_like(kcarry_ref)

        kb = k.astype(BF16).astype(F32)
        lane = lax.broadcasted_iota(jnp.int32, (1, LANES), 1)
        kmax = kcarry_ref[...]
        for hh in range(n_heads):
            kh = kb[:, hh * HEAD_DIM:(hh + 1) * HEAD_DIM]
            n2 = jnp.max(jnp.sum(kh * kh, axis=1, keepdims=True), axis=0, keepdims=True)
            kmax = jnp.where(lane == hh, jnp.maximum(kmax, jnp.sqrt(n2)), kmax)
        kcarry_ref[...] = kmax
        kmax_ref[...] = kmax

        fl = _dot(h, wf_ref[...]) + bf_ref[...]
        lf = jnp.minimum(fl, 0.0) - jnp.log(1.0 + jnp.exp(-jnp.abs(fl)))
        r = lax.broadcasted_iota(jnp.int32, (tm, tm), 0)
        c = lax.broadcasted_iota(jnp.int32, (tm, tm), 1)
        tri = jnp.where(c <= r, 1.0, 0.0).astype(BF16)
        p0 = lf.astype(BF16)
        r1 = lf - p0.astype(F32)
        p1 = r1.astype(BF16)
        p2 = (r1 - p1.astype(F32)).astype(BF16)
        csum = _dot(tri, p0) + _dot(tri, p1) + _dot(tri, p2) + carry_ref[...]
        carry_ref[...] = csum[tm - 1:tm, :]
        c2 = csum * LOG2E
        ct_ref[...] = c2.T[:n_heads, :]
        for hh in range(n_heads):
            cs_ref[hh] = jnp.broadcast_to(c2[:, hh:hh + 1], (tm, LANES))


def _qkv(x, g, w_qkv, layer, kind, tm, rope=None, fox=None):
    s, d = x.shape
    nb = s // tm
    n_heads = d // HEAD_DIM
    row = lambda i: (i, 0)
    col = lambda i: (0, i)
    const = lambda i: (0, 0)
    args = [x, g, w_qkv, w_qkv, w_qkv]
    specs = [pl.BlockSpec((tm, d), row), _resident((1, d), const)]
    specs += [_resident((None, d, d), lambda i, c=c: (layer, 0, c)) for c in range(3)]
    out_shape = [jax.ShapeDtypeStruct((d, s), BF16), jax.ShapeDtypeStruct((s, d), BF16),
                 jax.ShapeDtypeStruct((nb, d, tm), BF16)]
    out_specs = [pl.BlockSpec((d, tm), col), pl.BlockSpec((tm, d), row),
                 pl.BlockSpec((None, d, tm), lambda i: (i, 0, 0))]
    scratch = []
    if kind == 1:
        args += list(rope)
        specs += [pl.BlockSpec((ROT_HALF, tm), col)] * 2 + [pl.BlockSpec((tm, HEAD_DIM), row)] * 3
    if kind == 2:
        args += list(fox)
        specs += [_resident((d, LANES), const), _resident((1, LANES), const)]
        out_shape += [jax.ShapeDtypeStruct((n_heads, s, LANES), F32),
                      jax.ShapeDtypeStruct((n_heads, s), F32),
                      jax.ShapeDtypeStruct((nb, 1, LANES), F32)]
        out_specs += [pl.BlockSpec((n_heads, tm, LANES), lambda i: (0, i, 0)),
                      pl.BlockSpec((n_heads, tm), col),
                      pl.BlockSpec((None, 1, LANES), lambda i: (i, 0, 0))]
        scratch = [pltpu.VMEM((1, LANES), F32)] * 2
    kern = functools.partial(_qkv_kernel, kind=kind, n_heads=n_heads)
    return pl.pallas_call(
        kern, grid=(nb,), in_specs=specs, out_specs=out_specs, out_shape=out_shape,
        scratch_shapes=scratch, compiler_params=_params(40 if tm <= ROW_TILE else 56, 1),
        name=f"qkv{kind}")(*args)


def _tile_iotas(tk, tq):
    return (lax.broadcasted_iota(jnp.int32, (tk, tq), 0),
            lax.broadcasted_iota(jnp.int32, (tk, tq), 1))


def _run_steps(step, n_tiles, set0, set1):
    def pair(i, carry):
        step(2 * i, set0, set1)
        step(2 * i + 1, set1, set0)
        return carry

    lax.fori_loop(0, lax.shift_right_logical(n_tiles - 1, 1), pair, 0)
    odd = lax.bitwise_and(n_tiles, 1) == 1

    @pl.when(odd)
    def _():
        step(n_tiles - 1, set0, set1, last=True)

    @pl.when(jnp.logical_not(odd))
    def _():
        step(n_tiles - 2, set0, set1)
        step(n_tiles - 1, set1, set0, last=True)


def _sb_kernel(qt_ref, k_ref, vt_ref, o_ref, acc_ref, run_ref):
    qi = pl.program_id(1)
    tq = qt_ref.shape[1]
    tk = vt_ref.shape[2]
    n_heads = acc_ref.shape[0]
    qt = qt_ref[...]
    d_i, c_i = _tile_iotas(CUM_BLOCK, CUM_BLOCK)
    later = jnp.where(c_i > d_i, 1.0, 0.0).astype(BF16)
    acc_ref[...] = jnp.zeros_like(acc_ref)
    run_ref[...] = jnp.zeros_like(run_ref)
    heads = [slice(a * HEAD_DIM, (a + 1) * HEAD_DIM) for a in range(n_heads)]
    blk = CUM_BLOCK
    assert tq == tk == 2 * blk
    dead = SB_NEG * LOG2E

    def key_block(z, vcols, acc, run):
        sp = jnp.maximum(z, 0.0) + jnp.log(1.0 + jnp.exp2(-jnp.abs(z))) * LOG2E
        after = _dot(later, sp.astype(BF16)) + run
        w = jnp.exp2(z - sp - after).astype(BF16)
        return acc + _dot(vcols, w), run + jnp.sum(sp, axis=0, keepdims=True)

    j_prev = jnp.maximum(qi - 1, 0)
    row0 = pl.multiple_of(qi * tk, tk)
    k_diag = k_ref[pl.ds(row0, tk), :]
    k_prev = k_ref[pl.ds(pl.multiple_of(j_prev * tk + blk, blk), blk), :]
    v_diag, v_prev = vt_ref[qi], vt_ref[j_prev]
    d_i8, c_i8 = _tile_iotas(blk, blk)
    tri = d_i8 < c_i8
    zs = []
    for hd in heads:
        z_lo = _dot(k_diag[:blk, hd], qt[hd])
        z_hi = jnp.where(tri, _dot(k_diag[blk:, hd], qt[hd, blk:]), dead)
        z_pv = jnp.where(qi > 0, _dot(k_prev[:, hd], qt[hd, :blk]), dead)
        zs.append((jnp.where(tri, z_lo[:, :blk], dead), z_lo[:, blk:], z_hi, z_pv))
    low = None
    for a, (z_a0, z_b1, z_b0, z_a1) in enumerate(zs):
        hd = heads[a]
        zero_acc = jnp.zeros((HEAD_DIM, blk), F32)
        zero_run = jnp.zeros((1, blk), F32)
        acc_b, run_b = key_block(z_b0, v_diag[hd, blk:], zero_acc, zero_run)
        acc_b, run_b = key_block(z_b1, v_diag[hd, :blk], acc_b, run_b)
        acc_a, run_a = key_block(z_a0, v_diag[hd, :blk], zero_acc, zero_run)
        acc_a, run_a = key_block(z_a1, v_prev[hd, blk:], acc_a, run_a)
        acc_ref[a] = jnp.concatenate([acc_a, acc_b], axis=1)
        run = jnp.concatenate([run_a, run_b], axis=1)
        run_ref[a] = run
        low = run if low is None else jnp.minimum(low, run)
    live = (jnp.min(low) < DEAD_LOG2).astype(jnp.int32)

    def body(carry):
        n, _ = carry
        j = qi - n
        kb = k_ref[pl.ds(pl.multiple_of(j * tk, tk), tk), :]
        vb = vt_ref[j]
        key, qry = _tile_iotas(tk, tq)
        taken = (n == 1) & (key >= blk) & (qry < blk)
        low = None
        for a, hd in enumerate(heads):
            z = jnp.where(taken, dead, _dot(kb[:, hd], qt[hd]))
            acc, run = acc_ref[a], run_ref[a]
            for b in reversed(range(tk // blk)):
                sl = slice(b * blk, (b + 1) * blk)
                acc, run = key_block(z[sl], vb[hd, sl], acc, run)
            acc_ref[a], run_ref[a] = acc, run
            low = run if low is None else jnp.minimum(low, run)
        return n + 1, (jnp.min(low) < DEAD_LOG2).astype(jnp.int32)

    lax.while_loop(lambda c: (c[0] <= qi) & (c[1] > 0), body, (jnp.int32(1), live))
    for a in range(n_heads):
        o_ref[:, heads[a]] = acc_ref[a].T.astype(o_ref.dtype)


def _softmax_pipeline(qi, score, vt_ref, bufs, m_ref, l_ref, acc_ref):
    set0, set1 = bufs
    n_streams = m_ref.shape[0]
    m_ref[...] = jnp.full_like(m_ref, NEG)
    l_ref[...] = jnp.zeros_like(l_ref)
    acc_ref[...] = jnp.zeros_like(acc_ref)

    def put_scores(j, masked, z_ref, t_ref):
        for a, z in enumerate(score(j, masked)):
            z_ref[a] = z
            t_ref[a] = jnp.max(z, axis=0, keepdims=True)

    def step(n, cur, oth, last=False):
        j = qi - n
        if not last:
            put_scores(j - 1, False, *oth)
        z_ref, t_ref = cur
        vb = vt_ref[j]
        for a in range(n_streams):
            m_old = m_ref[a]
            m_new = jnp.maximum(m_old, t_ref[a])
            alpha = jnp.exp2(m_old - m_new)
            p = jnp.exp2(z_ref[a] - m_new)
            l_ref[a] = alpha * l_ref[a] + jnp.sum(p, axis=0, keepdims=True)
            m_ref[a] = m_new
            acc_ref[a] = alpha * acc_ref[a] + _dot(vb, p.astype(BF16))

    put_scores(qi, True, *set0)
    _run_steps(step, qi + 1, set0, set1)


def _diff_kernel(qt_ref, k_ref, vt_ref, lam_ref, g_ref, o_ref, m_ref, l_ref, acc_ref, *bufs, lam_init):
    qi = pl.program_id(1)
    tq = qt_ref.shape[1]
    tk = vt_ref.shape[2]
    qt = qt_ref[...]

    def scores(j, masked):
        kb = k_ref[pl.ds(pl.multiple_of(j * tk, tk), tk), :]
        zs = []
        for a in range(2):
            sl = slice(a * HEAD_DIM, (a + 1) * HEAD_DIM)
            z = _dot(kb[:, sl], qt[sl])
            if masked:
                key, qry = _tile_iotas(tk, tq)
                z = jnp.where(key <= qry, z, NEG)
            zs.append(z)
        return zs

    _softmax_pipeline(qi, scores, vt_ref, (bufs[:2], bufs[2:]), m_ref, l_ref, acc_ref)
    lp = lam_ref[...]
    lam = (jnp.exp(jnp.sum(lp[0:1] * lp[1:2], axis=-1, keepdims=True))
           - jnp.exp(jnp.sum(lp[2:3] * lp[3:4], axis=-1, keepdims=True)) + lam_init)
    o = acc_ref[0] * (1.0 / l_ref[0]) - lam * (acc_ref[1] * (1.0 / l_ref[1]))
    o_ref[...] = (_rms(o.T, g_ref[...]) * (1.0 - lam_init)).astype(o_ref.dtype)


def _fox_kernel(kmax_ref, qt_ref, k_ref, vt_ref, cs_ref, ct_ref, o_ref, m_ref, l_ref, acc_ref):
    n_heads = m_ref.shape[0]
    h0 = pl.program_id(0) * n_heads
    qi = pl.program_id(1)
    tq = qt_ref.shape[1]
    tk = vt_ref.shape[2]
    qt = qt_ref[...]
    ct_all = ct_ref[...]
    head = lax.broadcasted_iota(jnp.int32, ct_all.shape, 0)
    heads = [slice(a * HEAD_DIM, (a + 1) * HEAD_DIM) for a in range(n_heads)]
    ct, q_norm = [], []
    for a in range(n_heads):
        ct.append(jnp.sum(jnp.where(head == h0 + a, ct_all, 0.0), axis=0, keepdims=True))
        qf = qt[heads[a]].astype(F32)
        q_norm.append(jnp.sqrt(jnp.sum(qf * qf, axis=0, keepdims=True)))
    m_ref[...] = jnp.full_like(m_ref, NEG)
    l_ref[...] = jnp.zeros_like(l_ref)
    acc_ref[...] = jnp.zeros_like(acc_ref)

    def scores(j, masked):
        rows = pl.ds(pl.multiple_of(j * tk, tk), tk)
        kb = k_ref[rows, :]
        zs = []
        for a in range(n_heads):
            cs = jnp.concatenate([cs_ref[a, rows, :]] * (tq // LANES), axis=1)
            z = _dot(kb[:, heads[a]], qt[heads[a]]) + ct[a] - cs
            if masked:
                key, qry = _tile_iotas(tk, tq)
                z = jnp.where(key <= qry, z, NEG)
            zs.append(z)
        return zs

    def consume(zs, j):
        vb = vt_ref[j]
        jn = jnp.maximum(j - 1, 0)
        end = pl.ds(pl.multiple_of((jn + 1) * tk - 8, 8), 8)
        gap = None
        for a, z in enumerate(zs):
            m_old = m_ref[a]
            m_new = jnp.maximum(m_old, jnp.max(z, axis=0, keepdims=True))
            alpha = jnp.exp2(m_old - m_new)
            p = jnp.exp2(z - m_new)
            l_ref[a] = alpha * l_ref[a] + jnp.sum(p, axis=0, keepdims=True)
            acc_ref[a] = alpha * acc_ref[a] + _dot(vb[heads[a]], p.astype(BF16))
            m_ref[a] = m_new
            c_end = jnp.concatenate([cs_ref[a, end, :][7:8, :]] * (tq // LANES), axis=1)
            bound = q_norm[a] * kmax_ref[jn, h0 + a] + ct[a] - c_end + BOUND_SLACK
            gap = bound - m_new if gap is None else jnp.maximum(gap, bound - m_new)
        return (jnp.max(gap) > -DEAD_LOG2).astype(jnp.int32)

    j_prev = jnp.maximum(qi - 1, 0)
    z_diag = scores(qi, True)
    z_prev = [jnp.where(qi > 0, z, NEG) for z in scores(j_prev, False)]
    consume(z_diag, qi)
    live = consume(z_prev, j_prev)

    def body(carry):
        n, _ = carry
        return n + 1, consume(scores(qi - n, False), qi - n)

    lax.while_loop(lambda c: (c[0] <= qi) & (c[1] > 0), body, (jnp.int32(2), live))
    for a in range(n_heads):
        o_ref[:, heads[a]] = (acc_ref[a] * (1.0 / l_ref[a])).T.astype(o_ref.dtype)


def _attention(kind, qt, k, vt, extra, layer):
    d, s = qt.shape
    nb, _, tk = vt.shape
    tq = tk
    hw = (SB_HEADS if kind == 0 else 2) * HEAD_DIM
    n_heads = d // hw
    per_head = pl.BlockSpec if kind == 2 else _resident
    specs = [pl.BlockSpec((hw, tq), lambda h, i: (h, i)),
             per_head((s, hw), lambda h, i: (0, h)),
             per_head((nb, hw, tk), lambda h, i: (0, h, 0))]
    ns = hw // HEAD_DIM
    dv = hw if kind == 1 else HEAD_DIM
    stats = [pltpu.VMEM((ns, 1, tq), F32)] * 2 + [pltpu.VMEM((ns, dv, tq), F32)]
    args = [qt, k, vt]
    if kind == 0:
        kern, vmem = _sb_kernel, 48
        scratch = [pltpu.VMEM((ns, HEAD_DIM, tq), F32), pltpu.VMEM((ns, 1, tq), F32)]
    elif kind == 1:
        lam_init = 0.8 - 0.6 * math.exp(-0.3 * layer)
        kern, vmem = functools.partial(_diff_kernel, lam_init=lam_init), 58
        args += list(extra)
        specs += [pl.BlockSpec((4, HEAD_DIM), lambda h, i: (0, 0)),
                  pl.BlockSpec((1, hw), lambda h, i: (0, 0))]
        scratch = stats + [pltpu.VMEM((ns, tk, tq), F32), pltpu.VMEM((ns, 1, tq), F32)] * 2
    else:
        kern, vmem = _fox_kernel, 58
        cs, ct, kmax = extra
        args = [kmax] + args + [cs, ct]
        specs = [pl.BlockSpec(memory_space=pltpu.SMEM)] + specs
        specs += [_resident((ns, s, LANES), lambda h, i: (h, 0, 0)),
                  pl.BlockSpec((d // HEAD_DIM, tq), lambda h, i: (0, i))]
        scratch = stats
    return pl.pallas_call(
        kern, grid=(n_heads, s // tq), in_specs=specs,
        out_specs=pl.BlockSpec((tq, hw), lambda h, i: (i, h)),
        out_shape=jax.ShapeDtypeStruct((s, d), BF16), scratch_shapes=scratch,
        compiler_params=_params(vmem, 2), name=f"attn{kind}")(*args)


def _rope_tables(s):
    inv = ROPE_THETA ** (-jnp.arange(ROT_HALF, dtype=F32) * 2.0 / ROT_DIM)
    ang = jnp.arange(s).astype(F32)[:, None] * inv[None, :]
    cos, sin = jnp.cos(ang), jnp.sin(ang)
    pad = HEAD_DIM - ROT_DIM
    zeros = jnp.zeros_like(sin)
    cos_f = jnp.concatenate([cos, cos, jnp.ones((s, pad), F32)], axis=1)
    sin_a = jnp.concatenate([-sin, zeros, jnp.zeros((s, pad), F32)], axis=1)
    sin_b = jnp.concatenate([zeros, sin, jnp.zeros((s, pad), F32)], axis=1)
    return cos.T, sin.T, cos_f, sin_a, sin_b


def kernel(x, norm_g, final_g, ffn_w_gate, ffn_w_up, ffn_w_down, w_qkv, w_o,
           diff_lambda, diff_subln_g, fox_w_f, fox_b_f):
    b, s, d = x.shape
    assert b == 1 and s % DENSE_TILE == 0 and d % (2 * HEAD_DIM) == 0
    depth = w_qkv.shape[0]
    n_heads = d // HEAD_DIM
    xs = x.reshape(s, d)
    for i in range(depth):
        kind, j = i % N_MIXERS, i // N_MIXERS
        xs = _ffn(xs, norm_g[i, 0][None], ffn_w_gate, ffn_w_up, ffn_w_down, i, 0)
        rope = fox = None
        if kind == 1:
            rope = _rope_tables(s)
        if kind == 2:
            wf = jnp.zeros((d, LANES), F32).at[:, :n_heads].set(fox_w_f[j]).astype(BF16)
            bf = jnp.zeros((1, LANES), F32).at[0, :n_heads].set(fox_b_f[j])
            fox = (wf, bf)
        outs = _qkv(xs, norm_g[i, 1][None], w_qkv, i, kind,
                    DENSE_TILE if kind == 1 else ROW_TILE, rope=rope, fox=fox)
        extra = ()
        if kind == 1:
            extra = (diff_lambda[j], diff_subln_g[j][None])
        if kind == 2:
            extra = (outs[3], outs[4], outs[5].reshape(-1, LANES))
        mix = _attention(kind, outs[0], outs[1], outs[2], extra, i)
        xs = _ffn(xs, norm_g[i, 2][None], ffn_w_gate, ffn_w_up, ffn_w_down, i, 1,
                  proj=(mix, w_o),
                  final_g=final_g[None] if i == depth - 1 else None)
    return xs.reshape(b, s, d)
```

```python
import functools
import math

import jax
import jax.numpy as jnp
from jax import lax
from jax.experimental import pallas as pl
from jax.experimental.pallas import tpu as pltpu

F32 = jnp.float32
BF16 = jnp.bfloat16

HEAD_DIM = 128
N_MIXERS = 3
ROPE_THETA = 500000.0
ROT_DIM = HEAD_DIM // 4
ROT_HALF = ROT_DIM // 2
EPS = 1e-6
NEG = -1e30
SB_NEG = -1e4
LOG2E = 1.4426950408889634
Q_SCALE = HEAD_DIM ** -0.5 * LOG2E

LANES = 128
MXU_DIM = 256
VMEM_BYTES = 64 * 1024 * 1024

ROW_TILE = 512
DENSE_TILE = 1024
SB_HEADS = 4
FFN_CHUNK = 256
CUM_BLOCK = MXU_DIM
DEAD_LOG2 = 160.0
BOUND_SLACK = 1.0


def _dot(a, b):
    return jnp.dot(a, b, preferred_element_type=F32)


def _dot_tn(a, b):
    return lax.dot_general(a, b, (((0,), (1,)), ((), ())), preferred_element_type=F32)


def _rms(x, g):
    return x * lax.rsqrt(jnp.mean(x * x, axis=-1, keepdims=True) + EPS) * g


def _params(vmem_mib, n_axes):
    return pltpu.CompilerParams(
        dimension_semantics=("arbitrary",) * n_axes,
        vmem_limit_bytes=vmem_mib * 1024 * 1024)


def _resident(shape, index_map):
    return pl.BlockSpec(shape, index_map, pipeline_mode=pl.Buffered(1))


def _ffn_kernel(*refs, has_proj, has_final, n_chunks):
    it = iter(refs)
    x_ref = next(it)
    if has_proj:
        mix_ref, wo_ref = next(it), next(it)
    g_ref, wg_ref, wu_ref, wd_ref = next(it), next(it), next(it), next(it)
    if has_final:
        fg_ref = next(it)
    o_ref, a_ref = next(it), next(it)

    x = x_ref[...]
    if has_proj:
        x = x + _dot(mix_ref[...], wo_ref[...].astype(BF16))
    h = _rms(x, g_ref[...]).astype(BF16)
    for c in range(n_chunks):
        cols = slice(c * FFN_CHUNK, (c + 1) * FFN_CHUNK)
        gate = _dot(h, wg_ref[:, cols].astype(BF16))
        up = _dot(h, wu_ref[:, cols].astype(BF16))
        a_ref[:, cols] = (gate * jax.nn.sigmoid(gate) * up).astype(BF16)
    y = x + 0.5 * _dot(a_ref[...], wd_ref[...].astype(BF16))
    if has_final:
        y = _rms(y, fg_ref[...])
    o_ref[...] = y


def _ffn(x, g, wg, wu, wd, layer, which, proj=None, final_g=None):
    s, d = x.shape
    f = wd.shape[-2]
    tm = ROW_TILE
    row = lambda i: (i, 0)
    const = lambda i: (0, 0)
    pick = lambda i: (layer, which, 0, 0)
    args, specs = [x], [pl.BlockSpec((tm, d), row)]
    if proj is not None:
        mix, wo = proj
        args += [mix, wo]
        specs += [pl.BlockSpec((tm, d), row), _resident((None, d, d), lambda i: (layer, 0, 0))]
    args += [g, wg, wu, wd]
    specs += [_resident((1, d), const), _resident((None, None, d, f), pick),
              _resident((None, None, d, f), pick), _resident((None, None, f, d), pick)]
    if final_g is not None:
        args.append(final_g)
        specs.append(_resident((1, d), const))
    kern = functools.partial(_ffn_kernel, has_proj=proj is not None, has_final=final_g is not None,
                             n_chunks=f // FFN_CHUNK)
    return pl.pallas_call(
        kern, grid=(s // tm,), in_specs=specs, out_specs=pl.BlockSpec((tm, d), row),
        out_shape=jax.ShapeDtypeStruct((s, d), F32),
        scratch_shapes=[pltpu.VMEM((tm, f), BF16)],
        compiler_params=_params(58, 1), name="ffn")(*args)


def _qkv_kernel(*refs, kind, n_heads):
    it = iter(refs)
    x_ref, g_ref, wq_ref, wk_ref, wv_ref = (next(it) for _ in range(5))
    if kind == 1:
        cost_ref, sint_ref, cosf_ref, sina_ref, sinb_ref = (next(it) for _ in range(5))
    if kind == 2:
        wf_ref, bf_ref = next(it), next(it)
    qt_ref, k_ref, vt_ref = next(it), next(it), next(it)
    if kind == 2:
        cs_ref, ct_ref, kmax_ref, carry_ref, kcarry_ref = (next(it) for _ in range(5))

    h = _rms(x_ref[...], g_ref[...]).astype(BF16)
    tm = h.shape[0]
    if kind == 2:
        @pl.when(pl.program_id(0) == 0)
        def _():
            carry_ref[...] = jnp.zeros_like(carry_ref)
            kcarry_ref[...] = jnp.zeros_like(kcarry_ref)

        fl = _dot(h, wf_ref[...]) + bf_ref[...]
    qt = _dot_tn(wq_ref[...].astype(BF16), h) * Q_SCALE
    if kind == 2:
        lf = jnp.minimum(fl, 0.0) - jnp.log(1.0 + jnp.exp(-jnp.abs(fl)))
        r = lax.broadcasted_iota(jnp.int32, (tm, tm), 0)
        c = lax.broadcasted_iota(jnp.int32, (tm, tm), 1)
        tri = jnp.where(c <= r, 1.0, 0.0).astype(BF16)
        p0 = lf.astype(BF16)
        r1 = lf - p0.astype(F32)
        p1 = r1.astype(BF16)
        p2 = (r1 - p1.astype(F32)).astype(BF16)
        csum = _dot(tri, p0) + _dot(tri, p1) + _dot(tri, p2) + carry_ref[...]
        carry_ref[...] = csum[tm - 1:tm, :]
        c2 = csum * LOG2E
        ct_ref[...] = c2.T[:n_heads, :]
        for hh in range(n_heads):
            cs_ref[hh] = jnp.broadcast_to(c2[:, hh:hh + 1], (tm, LANES))
    k = _dot(h, wk_ref[...].astype(BF16))
    vt_ref[...] = _dot_tn(wv_ref[...].astype(BF16), h).astype(BF16)

    if kind != 1:
        qt_ref[...] = qt.astype(BF16)
        k_ref[...] = k.astype(BF16)
    else:
        cos_t, sin_t = cost_ref[...], sint_ref[...]
        cos_f, sin_a, sin_b = cosf_ref[...], sina_ref[...], sinb_ref[...]
        for hh in range(n_heads):
            r0 = hh * HEAD_DIM
            x1, x2 = qt[r0:r0 + ROT_HALF], qt[r0 + ROT_HALF:r0 + ROT_DIM]
            qt_ref[r0:r0 + ROT_HALF, :] = (x1 * cos_t - x2 * sin_t).astype(BF16)
            qt_ref[r0 + ROT_HALF:r0 + ROT_DIM, :] = (x2 * cos_t + x1 * sin_t).astype(BF16)
            qt_ref[r0 + ROT_DIM:r0 + HEAD_DIM, :] = qt[r0 + ROT_DIM:r0 + HEAD_DIM].astype(BF16)
            kh = k[:, r0:r0 + HEAD_DIM]
            rot = (kh * cos_f + pltpu.roll(kh, HEAD_DIM - ROT_HALF, 1) * sin_a
                   + pltpu.roll(kh, ROT_HALF, 1) * sin_b)
            k_ref[:, r0:r0 + HEAD_DIM] = rot.astype(BF16)

    if kind == 2:
        kb = k.astype(BF16).astype(F32)
        lane = lax.broadcasted_iota(jnp.int32, (1, LANES), 1)
        kmax = kcarry_ref[...]
        for hh in range(n_heads):
            kh = kb[:, hh * HEAD_DIM:(hh + 1) * HEAD_DIM]
            n2 = jnp.max(jnp.sum(kh * kh, axis=1, keepdims=True), axis=0, keepdims=True)
            kmax = jnp.where(lane == hh, jnp.maximum(kmax, jnp.sqrt(n2)), kmax)
        kcarry_ref[...] = kmax
        kmax_ref[...] = kmax


def _qkv(x, g, w_qkv, layer, kind, tm, rope=None, fox=None):
    s, d = x.shape
    nb = s // tm
    n_heads = d // HEAD_DIM
    row = lambda i: (i, 0)
    col = lambda i: (0, i)
    const = lambda i: (0, 0)
    args = [x, g, w_qkv, w_qkv, w_qkv]
    specs = [pl.BlockSpec((tm, d), row), _resident((1, d), const)]
    specs += [_resident((None, d, d), lambda i, c=c: (layer, 0, c)) for c in range(3)]
    out_shape = [jax.ShapeDtypeStruct((d, s), BF16), jax.ShapeDtypeStruct((s, d), BF16),
                 jax.ShapeDtypeStruct((nb, d, tm), BF16)]
    out_specs = [pl.BlockSpec((d, tm), col), pl.BlockSpec((tm, d), row),
                 pl.BlockSpec((None, d, tm), lambda i: (i, 0, 0))]
    scratch = []
    if kind == 1:
        args += list(rope)
        specs += [pl.BlockSpec((ROT_HALF, tm), col)] * 2 + [pl.BlockSpec((tm, HEAD_DIM), row)] * 3
    if kind == 2:
        args += list(fox)
        specs += [_resident((d, LANES), const), _resident((1, LANES), const)]
        out_shape += [jax.ShapeDtypeStruct((n_heads, s, LANES), F32),
                      jax.ShapeDtypeStruct((n_heads, s), F32),
                      jax.ShapeDtypeStruct((nb, 1, LANES), F32)]
        out_specs += [pl.BlockSpec((n_heads, tm, LANES), lambda i: (0, i, 0)),
                      pl.BlockSpec((n_heads, tm), col),
                      pl.BlockSpec((None, 1, LANES), lambda i: (i, 0, 0))]
        scratch = [pltpu.VMEM((1, LANES), F32)] * 2
    kern = functools.partial(_qkv_kernel, kind=kind, n_heads=n_heads)
    return pl.pallas_call(
        kern, grid=(nb,), in_specs=specs, out_specs=out_specs, out_shape=out_shape,
        scratch_shapes=scratch, compiler_params=_params(40 if tm <= ROW_TILE else 56, 1),
        name=f"qkv{kind}")(*args)


def _tile_iotas(tk, tq):
    return (lax.broadcasted_iota(jnp.int32, (tk, tq), 0),
            lax.broadcasted_iota(jnp.int32, (tk, tq), 1))


def _run_steps(step, n_tiles, set0, set1):
    def pair(i, carry):
        step(2 * i, set0, set1)
        step(2 * i + 1, set1, set0)
        return carry

    lax.fori_loop(0, lax.shift_right_logical(n_tiles - 1, 1), pair, 0)
    odd = lax.bitwise_and(n_tiles, 1) == 1

    @pl.when(odd)
    def _():
        step(n_tiles - 1, set0, set1, last=True)

    @pl.when(jnp.logical_not(odd))
    def _():
        step(n_tiles - 2, set0, set1)
        step(n_tiles - 1, set1, set0, last=True)


def _sb_kernel(qt_ref, k_ref, vt_ref, o_ref, acc_ref, run_ref):
    qi = pl.program_id(1)
    tq = qt_ref.shape[1]
    tk = vt_ref.shape[2]
    n_heads = acc_ref.shape[0]
    qt = qt_ref[...]
    d_i, c_i = _tile_iotas(CUM_BLOCK, CUM_BLOCK)
    later = jnp.where(c_i > d_i, 1.0, 0.0).astype(BF16)
    acc_ref[...] = jnp.zeros_like(acc_ref)
    run_ref[...] = jnp.zeros_like(run_ref)
    heads = [slice(a * HEAD_DIM, (a + 1) * HEAD_DIM) for a in range(n_heads)]
    blk = CUM_BLOCK
    assert tq == tk == 2 * blk
    dead = SB_NEG * LOG2E

    def key_block(z, vcols, acc, run):
        sp = jnp.maximum(z, 0.0) + jnp.log(1.0 + jnp.exp2(-jnp.abs(z))) * LOG2E
        after = _dot(later, sp.astype(BF16)) + run
        w = jnp.exp2(z - sp - after).astype(BF16)
        return acc + _dot(vcols, w), run + jnp.sum(sp, axis=0, keepdims=True)

    j_prev = jnp.maximum(qi - 1, 0)
    row0 = pl.multiple_of(qi * tk, tk)
    k_diag = k_ref[pl.ds(row0, tk), :]
    k_prev = k_ref[pl.ds(pl.multiple_of(j_prev * tk + blk, blk), blk), :]
    v_diag, v_prev = vt_ref[qi], vt_ref[j_prev]
    d_i8, c_i8 = _tile_iotas(blk, blk)
    tri = d_i8 < c_i8
    zs = []
    for hd in heads:
        z_lo = _dot(k_diag[:blk, hd], qt[hd])
        z_hi = jnp.where(tri, _dot(k_diag[blk:, hd], qt[hd, blk:]), dead)
        z_pv = jnp.where(qi > 0, _dot(k_prev[:, hd], qt[hd, :blk]), dead)
        zs.append((jnp.where(tri, z_lo[:, :blk], dead), z_lo[:, blk:], z_hi, z_pv))
    low = None
    for a, (z_a0, z_b1, z_b0, z_a1) in enumerate(zs):
        hd = heads[a]
        zero_acc = jnp.zeros((HEAD_DIM, blk), F32)
        zero_run = jnp.zeros((1, blk), F32)
        acc_b, run_b = key_block(z_b0, v_diag[hd, blk:], zero_acc, zero_run)
        acc_b, run_b = key_block(z_b1, v_diag[hd, :blk], acc_b, run_b)
        acc_a, run_a = key_block(z_a0, v_diag[hd, :blk], zero_acc, zero_run)
        acc_a, run_a = key_block(z_a1, v_prev[hd, blk:], acc_a, run_a)
        acc_ref[a] = jnp.concatenate([acc_a, acc_b], axis=1)
        run = jnp.concatenate([run_a, run_b], axis=1)
        run_ref[a] = run
        low = run if low is None else jnp.minimum(low, run)
    live = (jnp.min(low) < DEAD_LOG2).astype(jnp.int32)

    def body(carry):
        n, _ = carry
        j = qi - n
        kb = k_ref[pl.ds(pl.multiple_of(j * tk, tk), tk), :]
        vb = vt_ref[j]
        key, qry = _tile_iotas(tk, tq)
        taken = (n == 1) & (key >= blk) & (qry < blk)
        low = None
        for a, hd in enumerate(heads):
            z = jnp.where(taken, dead, _dot(kb[:, hd], qt[hd]))
            acc, run = acc_ref[a], run_ref[a]
            for b in reversed(range(tk // blk)):
                sl = slice(b * blk, (b + 1) * blk)
                acc, run = key_block(z[sl], vb[hd, sl], acc, run)
            acc_ref[a], run_ref[a] = acc, run
            low = run if low is None else jnp.minimum(low, run)
        return n + 1, (jnp.min(low) < DEAD_LOG2).astype(jnp.int32)

    lax.while_loop(lambda c: (c[0] <= qi) & (c[1] > 0), body, (jnp.int32(1), live))
    for a in range(n_heads):
        o_ref[:, heads[a]] = acc_ref[a].T.astype(o_ref.dtype)


def _softmax_pipeline(qi, score, vt_ref, bufs, m_ref, l_ref, acc_ref):
    set0, set1 = bufs
    n_streams = m_ref.shape[0]
    m_ref[...] = jnp.full_like(m_ref, NEG)
    l_ref[...] = jnp.zeros_like(l_ref)
    acc_ref[...] = jnp.zeros_like(acc_ref)

    def put_scores(j, masked, z_ref, t_ref):
        for a, z in enumerate(score(j, masked)):
            z_ref[a] = z
            t_ref[a] = jnp.max(z, axis=0, keepdims=True)

    def step(n, cur, oth, last=False):
        j = qi - n
        if not last:
            put_scores(j - 1, False, *oth)
        z_ref, t_ref = cur
        vb = vt_ref[j]
        for a in range(n_streams):
            m_old = m_ref[a]
            m_new = jnp.maximum(m_old, t_ref[a])
            alpha = jnp.exp2(m_old - m_new)
            p = jnp.exp2(z_ref[a] - m_new)
            l_ref[a] = alpha * l_ref[a] + jnp.sum(p, axis=0, keepdims=True)
            m_ref[a] = m_new
            acc_ref[a] = alpha * acc_ref[a] + _dot(vb, p.astype(BF16))

    put_scores(qi, True, *set0)
    _run_steps(step, qi + 1, set0, set1)


def _diff_kernel(qt_ref, k_ref, vt_ref, lam_ref, g_ref, o_ref, m_ref, l_ref, acc_ref, *bufs, lam_init):
    qi = pl.program_id(1)
    tq = qt_ref.shape[1]
    tk = vt_ref.shape[2]
    qt = qt_ref[...]

    def scores(j, masked):
        kb = k_ref[pl.ds(pl.multiple_of(j * tk, tk), tk), :]
        zs = []
        for a in range(2):
            sl = slice(a * HEAD_DIM, (a + 1) * HEAD_DIM)
            z = _dot(kb[:, sl], qt[sl])
            if masked:
                key, qry = _tile_iotas(tk, tq)
                z = jnp.where(key <= qry, z, NEG)
            zs.append(z)
        return zs

    _softmax_pipeline(qi, scores, vt_ref, (bufs[:2], bufs[2:]), m_ref, l_ref, acc_ref)
    lp = lam_ref[...]
    lam = (jnp.exp(jnp.sum(lp[0:1] * lp[1:2], axis=-1, keepdims=True))
           - jnp.exp(jnp.sum(lp[2:3] * lp[3:4], axis=-1, keepdims=True)) + lam_init)
    o = acc_ref[0] * (1.0 / l_ref[0]) - lam * (acc_ref[1] * (1.0 / l_ref[1]))
    o_ref[...] = (_rms(o.T, g_ref[...]) * (1.0 - lam_init)).astype(o_ref.dtype)


def _fox_kernel(kmax_ref, qt_ref, k_ref, vt_ref, cs_ref, ct_ref, o_ref, m_ref, l_ref, acc_ref):
    n_heads = m_ref.shape[0]
    h0 = pl.program_id(0) * n_heads
    qi = pl.program_id(1)
    tq = qt_ref.shape[1]
    tk = vt_ref.shape[2]
    qt = qt_ref[...]
    ct_all = ct_ref[...]
    head = lax.broadcasted_iota(jnp.int32, ct_all.shape, 0)
    heads = [slice(a * HEAD_DIM, (a + 1) * HEAD_DIM) for a in range(n_heads)]
    ct, q_norm = [], []
    for a in range(n_heads):
        ct.append(jnp.sum(jnp.where(head == h0 + a, ct_all, 0.0), axis=0, keepdims=True))
        qf = qt[heads[a]].astype(F32)
        q_norm.append(jnp.sqrt(jnp.sum(qf * qf, axis=0, keepdims=True)))
    m_ref[...] = jnp.full_like(m_ref, NEG)
    l_ref[...] = jnp.zeros_like(l_ref)
    acc_ref[...] = jnp.zeros_like(acc_ref)

    def scores(j, masked):
        rows = pl.ds(pl.multiple_of(j * tk, tk), tk)
        kb = k_ref[rows, :]
        zs = []
        for a in range(n_heads):
            cs = jnp.concatenate([cs_ref[a, rows, :]] * (tq // LANES), axis=1)
            z = _dot(kb[:, heads[a]], qt[heads[a]]) + ct[a] - cs
            if masked:
                key, qry = _tile_iotas(tk, tq)
                z = jnp.where(key <= qry, z, NEG)
            zs.append(z)
        return zs

    def consume(zs, j):
        vb = vt_ref[j]
        jn = jnp.maximum(j - 1, 0)
        end = pl.ds(pl.multiple_of((jn + 1) * tk - 8, 8), 8)
        gap = None
        for a, z in enumerate(zs):
            m_old = m_ref[a]
            m_new = jnp.maximum(m_old, jnp.max(z, axis=0, keepdims=True))
            alpha = jnp.exp2(m_old - m_new)
            p = jnp.exp2(z - m_new)
            l_ref[a] = alpha * l_ref[a] + jnp.sum(p, axis=0, keepdims=True)
            acc_ref[a] = alpha * acc_ref[a] + _dot(vb[heads[a]], p.astype(BF16))
            m_ref[a] = m_new
            c_end = jnp.concatenate([cs_ref[a, end, :][7:8, :]] * (tq // LANES), axis=1)
            bound = q_norm[a] * kmax_ref[jn, h0 + a] + ct[a] - c_end + BOUND_SLACK
            gap = bound - m_new if gap is None else jnp.maximum(gap, bound - m_new)
        return (jnp.max(gap) > -DEAD_LOG2).astype(jnp.int32)

    j_prev = jnp.maximum(qi - 1, 0)
    z_diag = scores(qi, True)
    z_prev = [jnp.where(qi > 0, z, NEG) for z in scores(j_prev, False)]
    consume(z_diag, qi)
    live = consume(z_prev, j_prev)

    def body(carry):
        n, _ = carry
        return n + 1, consume(scores(qi - n, False), qi - n)

    lax.while_loop(lambda c: (c[0] <= qi) & (c[1] > 0), body, (jnp.int32(2), live))
    for a in range(n_heads):
        o_ref[:, heads[a]] = (acc_ref[a] * (1.0 / l_ref[a])).T.astype(o_ref.dtype)


def _attention(kind, qt, k, vt, extra, layer):
    d, s = qt.shape
    nb, _, tk = vt.shape
    tq = tk
    hw = (SB_HEADS if kind == 0 else 2) * HEAD_DIM
    n_heads = d // hw
    per_head = pl.BlockSpec if kind == 2 else _resident
    specs = [pl.BlockSpec((hw, tq), lambda h, i: (h, i)),
             per_head((s, hw), lambda h, i: (0, h)),
             per_head((nb, hw, tk), lambda h, i: (0, h, 0))]
    ns = hw // HEAD_DIM
    dv = hw if kind == 1 else HEAD_DIM
    stats = [pltpu.VMEM((ns, 1, tq), F32)] * 2 + [pltpu.VMEM((ns, dv, tq), F32)]
    args = [qt, k, vt]
    if kind == 0:
        kern, vmem = _sb_kernel, 48
        scratch = [pltpu.VMEM((ns, HEAD_DIM, tq), F32), pltpu.VMEM((ns, 1, tq), F32)]
    elif kind == 1:
        lam_init = 0.8 - 0.6 * math.exp(-0.3 * layer)
        kern, vmem = functools.partial(_diff_kernel, lam_init=lam_init), 58
        args += list(extra)
        specs += [pl.BlockSpec((4, HEAD_DIM), lambda h, i: (0, 0)),
                  pl.BlockSpec((1, hw), lambda h, i: (0, 0))]
        scratch = stats + [pltpu.VMEM((ns, tk, tq), F32), pltpu.VMEM((ns, 1, tq), F32)] * 2
    else:
        kern, vmem = _fox_kernel, 58
        cs, ct, kmax = extra
        args = [kmax] + args + [cs, ct]
        specs = [pl.BlockSpec(memory_space=pltpu.SMEM)] + specs
        specs += [_resident((ns, s, LANES), lambda h, i: (h, 0, 0)),
                  pl.BlockSpec((d // HEAD_DIM, tq), lambda h, i: (0, i))]
        scratch = stats
    return pl.pallas_call(
        kern, grid=(n_heads, s // tq), in_specs=specs,
        out_specs=pl.BlockSpec((tq, hw), lambda h, i: (i, h)),
        out_shape=jax.ShapeDtypeStruct((s, d), BF16), scratch_shapes=scratch,
        compiler_params=_params(vmem, 2), name=f"attn{kind}")(*args)


def _rope_tables(s):
    inv = ROPE_THETA ** (-jnp.arange(ROT_HALF, dtype=F32) * 2.0 / ROT_DIM)
    ang = jnp.arange(s).astype(F32)[:, None] * inv[None, :]
    cos, sin = jnp.cos(ang), jnp.sin(ang)
    pad = HEAD_DIM - ROT_DIM
    zeros = jnp.zeros_like(sin)
    cos_f = jnp.concatenate([cos, cos, jnp.ones((s, pad), F32)], axis=1)
    sin_a = jnp.concatenate([-sin, zeros, jnp.zeros((s, pad), F32)], axis=1)
    sin_b = jnp.concatenate([zeros, sin, jnp.zeros((s, pad), F32)], axis=1)
    return cos.T, sin.T, cos_f, sin_a, sin_b


def kernel(x, norm_g, final_g, ffn_w_gate, ffn_w_up, ffn_w_down, w_qkv, w_o,
           diff_lambda, diff_subln_g, fox_w_f, fox_b_f):
    b, s, d = x.shape
    assert b == 1 and s % DENSE_TILE == 0 and d % (2 * HEAD_DIM) == 0
    depth = w_qkv.shape[0]
    n_heads = d // HEAD_DIM
    xs = x.reshape(s, d)
    for i in range(depth):
        kind, j = i % N_MIXERS, i // N_MIXERS
        xs = _ffn(xs, norm_g[i, 0][None], ffn_w_gate, ffn_w_up, ffn_w_down, i, 0)
        rope = fox = None
        if kind == 1:
            rope = _rope_tables(s)
        if kind == 2:
            wf = jnp.zeros((d, LANES), F32).at[:, :n_heads].set(fox_w_f[j]).astype(BF16)
            bf = jnp.zeros((1, LANES), F32).at[0, :n_heads].set(fox_b_f[j])
            fox = (wf, bf)
        outs = _qkv(xs, norm_g[i, 1][None], w_qkv, i, kind,
                    DENSE_TILE if kind == 1 else ROW_TILE, rope=rope, fox=fox)
        extra = ()
        if kind == 1:
            extra = (diff_lambda[j], diff_subln_g[j][None])
        if kind == 2:
            extra = (outs[3], outs[4], outs[5].reshape(-1, LANES))
        mix = _attention(kind, outs[0], outs[1], outs[2], extra, i)
        xs = _ffn(xs, norm_g[i, 2][None], ffn_w_gate, ffn_w_up, ffn_w_down, i, 1,
                  proj=(mix, w_o),
                  final_g=final_g[None] if i == depth - 1 else None)
    return xs.reshape(b, s, d)
```
